```python
import jax, jax.numpy as jnp
from jax import lax
import numpy as np


D_MODEL = 1024
BATCH = 4
SEQ = 8192
DEPTH = 2
DEC_BATCH = 8
DEC_SEQ = 4096
PAST_LEN = 128

A_HEADS = 8
A_KV_HEADS = 2
A_HEAD_DIM = 64
WINDOW = 128
B_HEADS = 8
B_Q_LORA = 256
B_KV_LORA = 128
B_NOPE = 64
B_ROPE = 32
B_V = 64
B_QK = B_NOPE + B_ROPE
ROPE_THETA = 10000.0
Q_BLOCK = 128
C_HEADS = 4
C_DK = 128
C_DV = 128
C_CHUNK = 64
D_GROUPS = 4
D_GROUP_DIM = 128
N_BRANCHES = 4
BRANCH_WIDTH = 512
N_EXPERTS = 16
EXPERT_FF = 1024
CAPACITY_FACTOR = 2
EPS = 1e-6
MASK_VALUE = -1e30
TINY = 1e-30

A_Q_W = A_HEADS * A_HEAD_DIM
A_KV_W = A_KV_HEADS * A_HEAD_DIM
B_KV_DOWN_W = B_KV_LORA + B_ROPE
C_K_W = C_HEADS * C_DK
C_V_W = C_HEADS * C_DV
D_W = D_GROUPS * D_GROUP_DIM
IN_SPLITS = (A_Q_W, A_KV_W, A_KV_W, B_Q_LORA, B_KV_DOWN_W, C_K_W, C_K_W, C_K_W, C_V_W, C_V_W, D_W)
IN_WIDTH = A_Q_W + 2 * A_KV_W + B_Q_LORA + B_KV_DOWN_W + 3 * C_K_W + 2 * C_V_W + D_W

kernel_name = 'hybrid_bidir_encoder_gated_branches'


def _rms_norm(x, g):
    xf = x.astype(jnp.float32)
    y = xf * lax.rsqrt(jnp.mean(xf * xf, axis=-1, keepdims=True) + EPS)
    return (y * g.astype(jnp.float32)).astype(x.dtype)


def _rope(x, pos):
    half = B_ROPE // 2
    inv = 1.0 / (ROPE_THETA ** (jnp.arange(half, dtype=jnp.float32) / half))
    ang = pos[:, None].astype(jnp.float32) * inv[None, :]
    cos = jnp.cos(ang)[:, None, :]
    sin = jnp.sin(ang)[:, None, :]
    xf = x.astype(jnp.float32)
    x1, x2 = xf[..., :half], xf[..., half:]
    return jnp.concatenate([x1 * cos - x2 * sin, x2 * cos + x1 * sin], axis=-1).astype(x.dtype)


def _windowed_gqa(q, k, v, sink):
    B, S, H, d = q.shape
    Hkv = k.shape[2]
    G = H // Hkv
    W = WINDOW
    nb = S // W
    qb = q.reshape(B, nb, W, Hkv, G, d)

    def band(t):
        tp = jnp.pad(t, ((0, 0), (W, W), (0, 0), (0, 0))).reshape(B, nb + 2, W, Hkv, d)
        return jnp.concatenate([tp[:, :-2], tp[:, 1:-1], tp[:, 2:]], axis=2)

    kw, vw = band(k), band(v)
    s = jnp.einsum('bnqhgd,bnkhd->bnhgqk', qb, kw).astype(jnp.float32) * (d ** -0.5)
    dist = jnp.abs(jnp.arange(3 * W)[None, :] - W - jnp.arange(W)[:, None])
    kpos = (jnp.arange(nb)[:, None] - 1) * W + jnp.arange(3 * W)[None, :]
    valid = (dist <= W)[None] & ((kpos >= 0) & (kpos < S))[:, None, :]
    slopes = (2.0 ** (-8.0 * jnp.arange(1, H + 1, dtype=jnp.float32) / H)).reshape(Hkv, G)
    s = s - slopes[:, :, None, None] * dist.astype(jnp.float32)
    s = jnp.where(valid[None, :, None, None], s, MASK_VALUE)
    sk = sink.astype(jnp.float32).reshape(Hkv, G)[None, None, :, :, None, None]
    m = jnp.maximum(jnp.max(s, axis=-1, keepdims=True), sk)
    p = jnp.exp(s - m)
    p = p / (jnp.sum(p, axis=-1, keepdims=True) + jnp.exp(sk - m))
    o = jnp.einsum('bnhgqk,bnkhd->bnqhgd', p.astype(v.dtype), vw)
    return o.reshape(B, S, H * d)


def _dense_attention_blocks(q, k, v):
    B, S, H, Dq = q.shape
    nb = S // Q_BLOCK
    qb = q.reshape(B, nb, Q_BLOCK, H, Dq).transpose(1, 0, 2, 3, 4)
    scale = Dq ** -0.5

    def one(qblk):
        s = jnp.einsum('bqhd,bkhd->bhqk', qblk, k).astype(jnp.float32) * scale
        p = jax.nn.softmax(s, axis=-1)
        return jnp.einsum('bhqk,bkhd->bqhd', p.astype(v.dtype), v)

    o = lax.map(one, qb)
    return o.transpose(1, 0, 2, 3, 4).reshape(B, S, H * v.shape[-1])


def _chunked_gated_scan(q, k, v, log_f):
    B, S, H, dk = q.shape
    dv = v.shape[-1]
    L = C_CHUNK
    N = S // L

    def to_chunks(t):
        return t.reshape(B, N, L, H, t.shape[-1]).transpose(1, 0, 3, 2, 4)

    qc, kc, vc = to_chunks(q), to_chunks(k), to_chunks(v)
    bc = jnp.cumsum(to_chunks(log_f), axis=3)
    causal = jnp.tril(jnp.ones((L, L), dtype=bool))[:, :, None]

    def step(state, inp):
        qb, kb, vb, bb = inp
        diff = bb[:, :, :, None, :] - bb[:, :, None, :, :]
        decay = jnp.exp(jnp.where(causal, diff, MASK_VALUE))
        scores = jnp.einsum('bhtk,bhtsk,bhsk->bhts', qb, decay, kb)
        o = jnp.einsum('bhts,bhsv->bhtv', scores, vb) + jnp.einsum('bhtk,bhkv->bhtv', qb * jnp.exp(bb), state)
        b_last = bb[:, :, -1:, :]
        new_state = state * jnp.exp(b_last[:, :, 0, :])[..., None] + jnp.einsum('bhsk,bhsv->bhkv', kb * jnp.exp(b_last - bb), vb)
        return new_state, o

    state0 = jnp.zeros((B, H, dk, dv), jnp.float32)
    _, o = lax.scan(step, state0, (qc, kc, vc, bc))
    return o.transpose(1, 0, 3, 2, 4).reshape(B, S, H, dv)


def _hgrn2_bidir(q, f_fwd, f_bwd, i, g, lb, out_norm_g):
    B, S, _ = q.shape

    def heads(t, d):
        return t.reshape(B, S, C_HEADS, d).astype(jnp.float32)

    qh = jax.nn.silu(heads(q, C_DK))
    vh = heads(i, C_DV)

    def gates(z, lb_dir):
        zh = heads(z, C_DK)
        lbh = lb_dir.astype(jnp.float32).reshape(C_HEADS, C_DK)
        f = lbh + (1.0 - lbh) * jax.nn.sigmoid(zh)
        log_f = jnp.log(jnp.maximum(f, TINY))
        key = (1.0 - lbh) * jax.nn.sigmoid(-zh)
        return key, log_f

    k_f, lf_f = gates(f_fwd, lb[0])
    k_b, lf_b = gates(f_bwd, lb[1])
    o_fwd = _chunked_gated_scan(qh, k_f, vh, lf_f)
    o_bwd = jnp.flip(_chunked_gated_scan(jnp.flip(qh, 1), jnp.flip(k_b, 1), jnp.flip(vh, 1), jnp.flip(lf_b, 1)), 1)
    o = _rms_norm(o_fwd + o_bwd, out_norm_g) * jax.nn.silu(heads(g, C_DV))
    return o.reshape(B, S, C_V_W).astype(q.dtype)


def _fourier_mix(u):
    B, S, _ = u.shape
    uf = u.reshape(B, S, D_GROUPS, D_GROUP_DIM).astype(jnp.float32)
    return jnp.fft.fft2(uf, axes=(1, 3), norm='ortho').real.reshape(B, S, D_W).astype(u.dtype)


def _expert_choice_ffn(h, w_router, w_e_gate, w_e_up, w_e_down):
    B, S, D = h.shape
    T = B * S
    cap = CAPACITY_FACTOR * T // N_EXPERTS
    ht = h.reshape(T, D)
    aff = jax.nn.softmax((ht @ w_router).astype(jnp.float32), axis=-1)
    gate, idx = lax.top_k(aff.T, cap)
    xe = jnp.take(ht, idx, axis=0)
    a = jnp.einsum('ecd,edf->ecf', xe, w_e_gate)
    u = jnp.einsum('ecd,edf->ecf', xe, w_e_up)
    ye = jnp.einsum('ecf,efd->ecd', jax.nn.silu(a) * u, w_e_down)
    ye = (ye.astype(jnp.float32) * gate[..., None]).astype(h.dtype)
    out = jnp.zeros_like(ht).at[idx.reshape(-1)].add(ye.reshape(-1, D))
    return out.reshape(B, S, D)


def _layer(x, lb, norm_mix_g, w_in, a_q_norm_g, a_k_norm_g, a_sink, b_q_a_norm_g, b_q_up, b_kv_a_norm_g, b_kv_up,
           b_q_norm_g, b_k_norm_g, c_out_norm_g, w_branch, w_gate, w_out, norm_ffn_g, w_router, w_e_gate, w_e_up, w_e_down):
    B, S, _ = x.shape
    h = _rms_norm(x, norm_mix_g)
    proj = h @ w_in
    split_points = [int(c) for c in np.cumsum(IN_SPLITS)[:-1]]
    (a_q, a_k, a_v, b_cq, b_ckv, c_q, c_f_fwd, c_f_bwd, c_i, c_g, d_u) = jnp.split(proj, split_points, axis=-1)

    qa = _rms_norm(a_q.reshape(B, S, A_HEADS, A_HEAD_DIM), a_q_norm_g)
    ka = _rms_norm(a_k.reshape(B, S, A_KV_HEADS, A_HEAD_DIM), a_k_norm_g)
    va = a_v.reshape(B, S, A_KV_HEADS, A_HEAD_DIM)
    o_a = _windowed_gqa(qa, ka, va, a_sink).astype(x.dtype)

    qb = (_rms_norm(b_cq, b_q_a_norm_g) @ b_q_up).reshape(B, S, B_HEADS, B_QK)
    ckv = b_ckv[..., :B_KV_LORA]
    k_pe = b_ckv[..., B_KV_LORA:]
    kv = (_rms_norm(ckv, b_kv_a_norm_g) @ b_kv_up).reshape(B, S, B_HEADS, B_NOPE + B_V)
    k_nope, vb = kv[..., :B_NOPE], kv[..., B_NOPE:]
    kb = jnp.concatenate([k_nope, jnp.broadcast_to(k_pe[:, :, None, :], (B, S, B_HEADS, B_ROPE))], axis=-1)
    qb = _rms_norm(qb, b_q_norm_g)
    kb = _rms_norm(kb, b_k_norm_g)
    pos = jnp.arange(S)
    qb = jnp.concatenate([qb[..., :B_NOPE], _rope(qb[..., B_NOPE:], pos)], axis=-1)
    kb = jnp.concatenate([kb[..., :B_NOPE], _rope(kb[..., B_NOPE:], pos)], axis=-1)
    o_b = _dense_attention_blocks(qb, kb, vb).astype(x.dtype)

    o_c = _hgrn2_bidir(c_q, c_f_fwd, c_f_bwd, c_i, c_g, lb, c_out_norm_g).astype(x.dtype)

    o_d = _fourier_mix(d_u)

    branches = (o_a, o_b, o_c, o_d)
    mixed = jax.nn.sigmoid((h @ w_gate[0]).astype(jnp.float32)) * (branches[0] @ w_branch[0]).astype(jnp.float32)
    for n in range(1, N_BRANCHES):
        mixed = mixed + jax.nn.sigmoid((h @ w_gate[n]).astype(jnp.float32)) * (branches[n] @ w_branch[n]).astype(jnp.float32)
    x = x + (mixed.astype(x.dtype) @ w_out)

    h2 = _rms_norm(x, norm_ffn_g)
    return x + _expert_choice_ffn(h2, w_router, w_e_gate, w_e_up, w_e_down)


def setup_inputs(seed: int = 0) -> dict:
    key = jax.random.key(seed)
    ks = jax.random.split(key, 32)
    f32 = jnp.float32

    def nrm(k, shape, scale):
        return jax.random.normal(k, shape, f32) * scale

    def gain(k, shape):
        return 1.0 + 0.02 * jax.random.normal(k, shape, f32)

    return {
        'x_prompt': nrm(ks[0], (BATCH, SEQ, D_MODEL), 1.0),
        'x_sample': nrm(ks[1], (DEC_BATCH, DEC_SEQ, D_MODEL), 1.0),
        'norm_mix_g': gain(ks[2], (DEPTH, D_MODEL)),
        'w_in': nrm(ks[3], (DEPTH, D_MODEL, IN_WIDTH), D_MODEL ** -0.5),
        'a_q_norm_g': gain(ks[4], (DEPTH, A_HEAD_DIM)),
        'a_k_norm_g': gain(ks[5], (DEPTH, A_HEAD_DIM)),
        'a_sink': nrm(ks[6], (DEPTH, A_HEADS), 0.5),
        'b_q_a_norm_g': gain(ks[7], (DEPTH, B_Q_LORA)),
        'b_q_up': nrm(ks[8], (DEPTH, B_Q_LORA, B_HEADS * B_QK), B_Q_LORA ** -0.5),
        'b_kv_a_norm_g': gain(ks[9], (DEPTH, B_KV_LORA)),
        'b_kv_up': nrm(ks[10], (DEPTH, B_KV_LORA, B_HEADS * (B_NOPE + B_V)), B_KV_LORA ** -0.5),
        'b_q_norm_g': gain(ks[11], (DEPTH, B_QK)),
        'b_k_norm_g': gain(ks[12], (DEPTH, B_QK)),
        'c_lb_logits': nrm(ks[13], (DEPTH, 2, C_K_W), 0.1),
        'c_out_norm_g': gain(ks[14], (DEPTH, C_DV)),
        'w_branch': nrm(ks[15], (DEPTH, N_BRANCHES, BRANCH_WIDTH, D_MODEL), BRANCH_WIDTH ** -0.5),
        'w_gate': nrm(ks[16], (DEPTH, N_BRANCHES, D_MODEL, D_MODEL), D_MODEL ** -0.5),
        'w_out': nrm(ks[17], (DEPTH, D_MODEL, D_MODEL), D_MODEL ** -0.5),
        'norm_ffn_g': gain(ks[18], (DEPTH, D_MODEL)),
        'w_router': nrm(ks[19], (DEPTH, D_MODEL, N_EXPERTS), D_MODEL ** -0.5),
        'w_e_gate': nrm(ks[20], (DEPTH, N_EXPERTS, D_MODEL, EXPERT_FF), D_MODEL ** -0.5),
        'w_e_up': nrm(ks[21], (DEPTH, N_EXPERTS, D_MODEL, EXPERT_FF), D_MODEL ** -0.5),
        'w_e_down': nrm(ks[22], (DEPTH, N_EXPERTS, EXPERT_FF, D_MODEL), EXPERT_FF ** -0.5),
    }


def reference(x_prompt, x_sample, norm_mix_g, w_in, a_q_norm_g, a_k_norm_g, a_sink, b_q_a_norm_g, b_q_up,
              b_kv_a_norm_g, b_kv_up, b_q_norm_g, b_k_norm_g, c_lb_logits, c_out_norm_g, w_branch, w_gate, w_out,
              norm_ffn_g, w_router, w_e_gate, w_e_up, w_e_down):
    lb_w = jax.nn.softmax(c_lb_logits.astype(jnp.float32), axis=0)
    lb_all = jnp.cumsum(lb_w, axis=0) - lb_w[0:1]
    y_prompt = x_prompt
    y_sample = x_sample
    for l in range(DEPTH):
        args = (lb_all[l], norm_mix_g[l], w_in[l], a_q_norm_g[l], a_k_norm_g[l], a_sink[l], b_q_a_norm_g[l],
                b_q_up[l], b_kv_a_norm_g[l], b_kv_up[l], b_q_norm_g[l], b_k_norm_g[l], c_out_norm_g[l],
                w_branch[l], w_gate[l], w_out[l], norm_ffn_g[l], w_router[l], w_e_gate[l], w_e_up[l], w_e_down[l])
        y_prompt = _layer(y_prompt, *args)
        y_sample = _layer(y_sample, *args)
    return (y_prompt, y_sample)
```

```python
import functools
import math

import numpy as np
import jax
import jax.numpy as jnp
from jax import lax
from jax.experimental import pallas as pl
from jax.experimental.pallas import tpu as pltpu

F32 = jnp.float32
BF16 = jnp.bfloat16

D_MODEL = 1024
DEPTH = 2
A_HEADS, A_KV_HEADS, A_HEAD_DIM, WINDOW = 8, 2, 64, 128
B_HEADS, B_Q_LORA, B_KV_LORA, B_NOPE, B_ROPE, B_V = 8, 256, 128, 64, 32, 64
B_QK = B_NOPE + B_ROPE
ROPE_THETA = 10000.0
C_HEADS, C_DK, C_DV = 4, 128, 128
D_GROUPS, D_GROUP_DIM = 4, 128
N_BRANCHES, BRANCH_WIDTH = 4, 512
N_EXPERTS, EXPERT_FF, CAPACITY_FACTOR = 16, 1024, 2
EPS = 1e-6
MASK_VALUE = -1e30
TINY = 1e-30

LANES = 128
BF16_SUBLANES = 16
V7X_VMEM_LIMIT = 56 * 1024 * 1024

TOKEN_TILE = 512
PAD_HEAD = 128
HGRN_CHUNK = 128
FFT_N2 = 128
FLASH_TQ, FLASH_TK = 256, 512
MOE_TILE = 1024
MOE_SUB = 256
FFN_ROWS = 512

OFF_AQ, OFF_CQ, OFF_CFF, OFF_CFB, OFF_CI, OFF_CG = 0, 512, 1024, 1536, 2048, 2560
OFF_AK, OFF_AV, OFF_BCQ, OFF_BCKV = 3072, 3200, 3328, 3584
PROJ_W = 3840


def _cparams(sem, vmem=V7X_VMEM_LIMIT):
    return pltpu.CompilerParams(dimension_semantics=sem, vmem_limit_bytes=vmem)


def _nt(a, b):
    return lax.dot_general(a, b, (((1,), (1,)), ((), ())), preferred_element_type=F32)


def _mm(a, b):
    return jnp.dot(a, b, preferred_element_type=F32)


def _rms(x, g):
    return x * lax.rsqrt(jnp.mean(x * x, axis=-1, keepdims=True) + EPS) * g


def _in_proj_kernel(x_ref, g_ref, w_ref, wu_ref, p_ref, u_ref):
    h = _rms(x_ref[...], g_ref[...]).astype(BF16)
    for c0 in range(0, PROJ_W, 512):
        c1 = min(c0 + 512, PROJ_W)
        p_ref[:, c0:c1] = _mm(h, w_ref[:, c0:c1]).astype(BF16)
    u_ref[...] = _mm(h, wu_ref[...]).astype(BF16)


def _in_proj(x2, g, w_main, w_u):
    T = x2.shape[0]
    tm = TOKEN_TILE
    return pl.pallas_call(
        _in_proj_kernel,
        grid=(T // tm,),
        in_specs=[
            pl.BlockSpec((tm, D_MODEL), lambda i: (i, 0)),
            pl.BlockSpec((1, D_MODEL), lambda i: (0, 0)),
            pl.BlockSpec((D_MODEL, PROJ_W), lambda i: (0, 0)),
            pl.BlockSpec((D_MODEL, 512), lambda i: (0, 0)),
        ],
        out_specs=[
            pl.BlockSpec((tm, PROJ_W), lambda i: (i, 0)),
            pl.BlockSpec((tm, 512), lambda i: (i, 0)),
        ],
        out_shape=[jax.ShapeDtypeStruct((T, PROJ_W), BF16), jax.ShapeDtypeStruct((T, 512), BF16)],
        compiler_params=_cparams(("parallel",)),
        name="in_proj",
    )(x2, g, w_main, w_u)


def _win_attn_kernel(q_ref, kp_ref, kc_ref, kn_ref, vp_ref, vc_ref, vn_ref, gq_ref, gk_ref, sink_ref, o_ref, *, nb):
    n = pl.program_id(1)
    W = WINDOW
    G = A_HEADS // A_KV_HEADS
    q = q_ref[...].astype(F32)
    kcat = jnp.concatenate([kp_ref[...], kc_ref[...], kn_ref[...]], axis=0).astype(F32)
    vcat = jnp.concatenate([vp_ref[...], vc_ref[...], vn_ref[...]], axis=0)
    gq = gq_ref[...]
    gk = gk_ref[...]
    row = lax.broadcasted_iota(jnp.int32, (G * W, 3 * W), 0)
    col = lax.broadcasted_iota(jnp.int32, (G * W, 3 * W), 1)
    qi = row % W
    dist = jnp.abs(col - W - qi)
    valid = dist <= W
    valid = valid & ((col >= W) | (n > 0)) & ((col < 2 * W) | (n < nb - 1))
    distf = dist.astype(F32)
    hrow = lax.broadcasted_iota(jnp.int32, (G * W, 1), 0) // W
    for g in range(A_KV_HEADS):
        qs = []
        for j in range(G):
            h = g * G + j
            qs.append(_rms(q[:, h * A_HEAD_DIM:(h + 1) * A_HEAD_DIM], gq))
        qg = (jnp.concatenate(qs, axis=0) * (A_HEAD_DIM ** -0.5)).astype(BF16)
        kg = _rms(kcat[:, g * A_HEAD_DIM:(g + 1) * A_HEAD_DIM], gk).astype(BF16)
        vg = vcat[:, g * A_HEAD_DIM:(g + 1) * A_HEAD_DIM]
        slope = jnp.exp2(-(hrow + (g * G + 1)).astype(F32))
        s = _nt(qg, kg) - slope * distf
        s = jnp.where(valid, s, MASK_VALUE)
        sk = sink_ref[g]
        m = jnp.maximum(jnp.max(s, axis=-1, keepdims=True), sk)
        p = jnp.exp(s - m)
        den = jnp.sum(p, axis=-1, keepdims=True) + jnp.exp(sk - m)
        o = _mm(p.astype(BF16), vg) / den
        for j in range(G):
            h = g * G + j
            o_ref[:, h * A_HEAD_DIM:(h + 1) * A_HEAD_DIM] = o[j * W:(j + 1) * W].astype(BF16)


def _win_attn(proj, gq, gk, sink_cols, B, S):
    T = B * S
    W = WINDOW
    nb = S // W
    kcol, vcol = OFF_AK // 128, OFF_AV // 128

    def cur(b, n):
        return b * nb + n

    def prev(b, n):
        return b * nb + jnp.maximum(n - 1, 0)

    def nxt(b, n):
        return b * nb + jnp.minimum(n + 1, nb - 1)

    def spec(rowf, c):
        return pl.BlockSpec((W, 128), lambda b, n: (rowf(b, n), c))

    return pl.pallas_call(
        functools.partial(_win_attn_kernel, nb=nb),
        grid=(B, nb),
        in_specs=[
            pl.BlockSpec((W, 512), lambda b, n: (cur(b, n), OFF_AQ // 512)),
            spec(prev, kcol), spec(cur, kcol), spec(nxt, kcol),
            spec(prev, vcol), spec(cur, vcol), spec(nxt, vcol),
            pl.BlockSpec((1, A_HEAD_DIM), lambda b, n: (0, 0)),
            pl.BlockSpec((1, A_HEAD_DIM), lambda b, n: (0, 0)),
            pl.BlockSpec((A_KV_HEADS, 4 * W, 1), lambda b, n: (0, 0, 0)),
        ],
        out_specs=pl.BlockSpec((W, 512), lambda b, n: (cur(b, n), 0)),
        out_shape=jax.ShapeDtypeStruct((T, 512), BF16),
        compiler_params=_cparams(("parallel", "parallel")),
        name="win_attn",
    )(proj, proj, proj, proj, proj, proj, proj, gq, gk, sink_cols)


def _rope_tables(S):
    half = B_ROPE // 2
    inv = 1.0 / (ROPE_THETA ** (np.arange(half, dtype=np.float32) / half))
    ang = np.arange(S, dtype=np.float32)[:, None] * inv[None, :]
    cos, sin = np.cos(ang), np.sin(ang)
    cosf = np.ones((S, PAD_HEAD), np.float32)
    sina = np.zeros((S, PAD_HEAD), np.float32)
    sinb = np.zeros((S, PAD_HEAD), np.float32)
    cosf[:, B_NOPE:B_NOPE + half] = cos
    cosf[:, B_NOPE + half:B_NOPE + 2 * half] = cos
    sina[:, B_NOPE:B_NOPE + half] = -sin
    sinb[:, B_NOPE + half:B_NOPE + 2 * half] = sin
    return jnp.asarray(cosf), jnp.asarray(sina), jnp.asarray(sinb)


def _mla_prep_kernel(cq_ref, ckv_ref, gqa_ref, gkva_ref, wq_ref, wk_ref, wv_ref, gq_ref, gk_ref,
                     cos_ref, sina_ref, sinb_ref, q_ref, k_ref, v_ref):
    half = B_ROPE // 2
    cosf, sina, sinb = cos_ref[...], sina_ref[...], sinb_ref[...]

    def rope(x):
        return x * cosf + pltpu.roll(x, LANES - half, 1) * sina + pltpu.roll(x, half, 1) * sinb

    def headnorm(x, g):
        return x * lax.rsqrt(jnp.sum(x * x, axis=-1, keepdims=True) * (1.0 / B_QK) + EPS) * g

    cq = _rms(cq_ref[...].astype(F32), gqa_ref[...]).astype(BF16)
    qall = _mm(cq, wq_ref[...])
    ckv = ckv_ref[...].astype(F32)
    c = _rms(ckv[:, :B_KV_LORA], gkva_ref[...]).astype(BF16)
    kpe = ckv[:, B_KV_LORA:]
    kall = _mm(c, wk_ref[...])
    v_ref[...] = _mm(c, wv_ref[...]).astype(BF16)
    gq, gk = gq_ref[...], gk_ref[...]
    for h in range(B_HEADS):
        sl = slice(h * PAD_HEAD, (h + 1) * PAD_HEAD)
        q_ref[:, sl] = (rope(headnorm(qall[:, sl], gq)) * (B_QK ** -0.5)).astype(BF16)
        k_ref[:, sl] = rope(headnorm(kall[:, sl] + kpe, gk)).astype(BF16)


def _mla_prep(proj, gqa, gkva, wq, wk, wv, gq, gk, tables, S):
    T = proj.shape[0]
    tm = TOKEN_TILE
    nper = S // tm
    HW = B_HEADS * PAD_HEAD
    cst = lambda i: (0, 0)
    tab = pl.BlockSpec((tm, PAD_HEAD), lambda i: (i % nper, 0))
    out = jax.ShapeDtypeStruct((T, HW), BF16)
    return pl.pallas_call(
        _mla_prep_kernel,
        grid=(T // tm,),
        in_specs=[
            pl.BlockSpec((tm, 256), lambda i: (i, OFF_BCQ // 256)),
            pl.BlockSpec((tm, 256), lambda i: (i, OFF_BCKV // 256)),
            pl.BlockSpec((1, B_Q_LORA), cst), pl.BlockSpec((1, B_KV_LORA), cst),
            pl.BlockSpec((B_Q_LORA, HW), cst), pl.BlockSpec((B_KV_LORA, HW), cst), pl.BlockSpec((B_KV_LORA, HW), cst),
            pl.BlockSpec((1, PAD_HEAD), cst), pl.BlockSpec((1, PAD_HEAD), cst),
            tab, tab, tab,
        ],
        out_specs=[pl.BlockSpec((tm, HW), lambda i: (i, 0))] * 3,
        out_shape=[out, out, out],
        compiler_params=_cparams(("parallel",)),
        name="mla_prep",
    )(proj, proj, gqa, gkva, wq, wk, wv, gq, gk, *tables)


def _flash_kernel(q_ref, k_ref, v_ref, o_ref, *, S):
    tq, tk = FLASH_TQ, FLASH_TK
    for hh in range(2):
        sl = slice(hh * PAD_HEAD, (hh + 1) * PAD_HEAD)
        q = q_ref[:, sl]

        def body(j, carry, sl=sl, q=q):
            m, l, acc = carry
            r0 = pl.multiple_of(j * tk, tk)
            k = k_ref[pl.ds(r0, tk), sl]
            v = v_ref[pl.ds(r0, tk), sl]
            s = _nt(q, k)
            mn = jnp.maximum(m, jnp.max(s, axis=-1, keepdims=True))
            a = jnp.exp(m - mn)
            p = jnp.exp(s - mn)
            l = l * a + jnp.sum(p, axis=-1, keepdims=True)
            acc = acc * a + _mm(p.astype(BF16), v)
            return mn, l, acc

        m0 = jnp.full((tq, 1), MASK_VALUE, F32)
        l0 = jnp.zeros((tq, 1), F32)
        a0 = jnp.zeros((tq, PAD_HEAD), F32)
        _, l, acc = lax.fori_loop(0, S // tk, body, (m0, l0, a0))
        o_ref[:, hh * B_V:(hh + 1) * B_V] = (acc[:, :B_V] / l).astype(BF16)


def _flash(q, k, v, B, S):
    T = B * S
    tq = FLASH_TQ
    nq = S // tq
    return pl.pallas_call(
        functools.partial(_flash_kernel, S=S),
        grid=(B, B_HEADS // 2, nq),
        in_specs=[
            pl.BlockSpec((tq, 2 * PAD_HEAD), lambda b, h, i: (b * nq + i, h)),
            pl.BlockSpec((S, 2 * PAD_HEAD), lambda b, h, i: (b, h)),
            pl.BlockSpec((S, 2 * PAD_HEAD), lambda b, h, i: (b, h)),
        ],
        out_specs=pl.BlockSpec((tq, 2 * B_V), lambda b, h, i: (b * nq + i, h)),
        out_shape=jax.ShapeDtypeStruct((T, B_HEADS * B_V), BF16),
        compiler_params=_cparams(("parallel", "parallel", "parallel")),
        name="mla_flash",
    )(q, k, v)


def _hgrn_consts(L, reverse):
    levels = []
    m = 1
    while m < L:
        levels.append(m)
        m *= 2
    nl = len(levels)
    t = np.arange(L)
    wall = np.zeros((nl + 2, L, L), np.float32)
    right = np.zeros((nl, L, 1), np.float32)
    mask = np.zeros((nl + 1, L, L), np.float32)
    for li, m in enumerate(levels):
        blk = t // (2 * m)
        mid = blk * 2 * m + m
        isr = t >= mid
        for tt in range(L):
            if isr[tt]:
                wall[li, tt, mid[tt]:tt + 1] = 1.0
            else:
                wall[li, tt, tt + 1:mid[tt]] = 1.0
        right[li, :, 0] = isr
        mask[li] = (blk[:, None] == blk[None, :]) & isr[:, None] & (~isr[None, :])
    mask[nl] = np.eye(L)
    wall[nl] = np.tril(np.ones((L, L)))
    wall[nl + 1] = np.triu(np.ones((L, L)), 1)
    if reverse:
        wall = wall[:, ::-1, ::-1]
        right = right[:, ::-1]
        mask = mask[:, ::-1, ::-1]
    return (jnp.asarray(np.ascontiguousarray(wall.reshape((nl + 2) * L, L)), BF16),
            jnp.asarray(np.ascontiguousarray(right)), jnp.asarray(np.ascontiguousarray(mask)), nl)


def _hgrn_kernel(*refs, L, nl, reverse):
    if reverse:
        q_ref, z_ref, v_ref, lb_ref, wall_ref, right_ref, mask_ref, of_ref, g_ref, gn_ref, o_ref, st_ref = refs
    else:
        q_ref, z_ref, v_ref, lb_ref, wall_ref, right_ref, mask_ref, o_ref, st_ref = refs

    @pl.when(pl.program_id(1) == 0)
    def _():
        st_ref[...] = jnp.zeros_like(st_ref)

    lb = lb_ref[...]
    z = z_ref[...].astype(F32)
    sig = jax.nn.sigmoid(z)
    lf = jnp.log(jnp.maximum(lb + (1.0 - lb) * sig, TINY))
    key = (1.0 - lb) * (1.0 - sig)
    qx = q_ref[...].astype(F32)
    qh = qx * jax.nn.sigmoid(qx)
    hi = lf.astype(BF16)
    lo = (lf - hi.astype(F32)).astype(BF16)
    wall = wall_ref[...]
    e_all = jnp.exp(_mm(wall, hi) + _mm(wall, lo))
    last = 0 if reverse else L - 1
    for h in range(C_HEADS):
        sl = slice(h * C_DK, (h + 1) * C_DK)
        qh_h, key_h, v_h = qh[:, sl], key[:, sl], v_ref[:, sl]
        scores = _nt(qh_h.astype(BF16), key_h.astype(BF16)) * mask_ref[nl]
        for li in range(nl):
            x = (jnp.where(right_ref[li] > 0.0, qh_h, key_h) * e_all[li * L:(li + 1) * L, sl]).astype(BF16)
            scores = scores + _nt(x, x) * mask_ref[li]
        o = _mm(scores.astype(BF16), v_h)
        ep = e_all[nl * L:(nl + 1) * L, sl]
        er = e_all[(nl + 1) * L:(nl + 2) * L, sl]
        st = st_ref[h]
        o = o + _nt((qh_h * ep).astype(BF16), st.astype(BF16))
        st_ref[h] = st * ep[last:last + 1] + _mm(v_h.T, (key_h * er).astype(BF16))
        if reverse:
            tot = o + of_ref[:, sl]
            o_ref[:, sl] = (_rms(tot, gn_ref[...]) * (lambda u: u * jax.nn.sigmoid(u))(g_ref[:, sl].astype(F32))).astype(BF16)
        else:
            o_ref[:, sl] = o


def _hgrn(proj, lb_dir, consts, B, S, reverse, o_fwd=None, gn=None):
    T = B * S
    L = HGRN_CHUNK
    nc = S // L
    wall, right, mask, nl = consts

    def rows(b, c):
        return b * nc + (nc - 1 - c if reverse else c)

    def pspec(off):
        return pl.BlockSpec((L, 512), lambda b, c: (rows(b, c), off // 512))

    cst2 = lambda b, c: (0, 0)
    cst3 = lambda b, c: (0, 0, 0)
    in_specs = [
        pspec(OFF_CQ), pspec(OFF_CFB if reverse else OFF_CFF), pspec(OFF_CI),
        pl.BlockSpec((1, 512), cst2),
        pl.BlockSpec(wall.shape, cst2), pl.BlockSpec(right.shape, cst3), pl.BlockSpec(mask.shape, cst3),
    ]
    args = [proj, proj, proj, lb_dir, wall, right, mask]
    if reverse:
        in_specs += [pl.BlockSpec((L, 512), lambda b, c: (rows(b, c), 0)), pspec(OFF_CG), pl.BlockSpec((1, C_DV), cst2)]
        args += [o_fwd, proj, gn]
    return pl.pallas_call(
        functools.partial(_hgrn_kernel, L=L, nl=nl, reverse=reverse),
        grid=(B, nc),
        in_specs=in_specs,
        out_specs=pl.BlockSpec((L, 512), lambda b, c: (rows(b, c), 0)),
        out_shape=jax.ShapeDtypeStruct((T, 512), BF16 if reverse else F32),
        scratch_shapes=[pltpu.VMEM((C_HEADS, C_DV, C_DK), F32)],
        compiler_params=_cparams(("parallel", "arbitrary")),
        name="hgrn_bwd" if reverse else "hgrn_fwd",
    )(*args)


def _fft_consts(S):
    N2 = FFT_N2
    N1 = S // N2
    d = np.arange(D_GROUP_DIM)
    ang = 2.0 * np.pi * ((d[:, None] * d[None, :]) % D_GROUP_DIM) / D_GROUP_DIM
    wc = np.concatenate([np.cos(ang), -np.sin(ang)], axis=1) / math.sqrt(D_GROUP_DIM)
    a = np.arange(N1)
    ang1 = 2.0 * np.pi * ((a[:, None] * a[None, :]) % N1) / N1
    c1, s1 = np.cos(ang1), np.sin(ang1)
    m1 = np.block([[c1, s1], [-s1, c1]])
    p1 = jnp.arange(N1, dtype=jnp.int32)[:, None, None]
    p2 = jnp.arange(N2, dtype=jnp.int32)[None, :, None]
    s2 = jnp.arange(N2, dtype=jnp.int32)[None, None, :]
    kk = (s2 * (p1 + N1 * p2)) % S
    th = kk.astype(F32) * (2.0 * math.pi / S)
    m2 = jnp.concatenate([jnp.cos(th), jnp.sin(th)], axis=2) * (1.0 / math.sqrt(S))
    return jnp.asarray(wc, BF16), jnp.asarray(m1, BF16), m2.astype(BF16), N1


def _fft1_kernel(u_ref, wc_ref, m1_ref, ar_ref, ai_ref, *, N1, ct):
    wc, m1 = wc_ref[...], m1_ref[...]
    for g in range(ct // LANES):
        sl = slice(g * LANES, (g + 1) * LANES)
        z = _mm(u_ref[:, sl], wc)
        zs = jnp.concatenate([z[:, :LANES], z[:, LANES:]], axis=0).astype(BF16)
        a = _mm(m1, zs)
        ar_ref[:, sl] = a[:N1].astype(BF16)
        ai_ref[:, sl] = a[N1:].astype(BF16)


def _fft2_kernel(ar_ref, ai_ref, m2_ref, o_ref):
    x = jnp.concatenate([ar_ref[0], ai_ref[0]], axis=0)
    o_ref[...] = _mm(m2_ref[0], x).astype(BF16)


def _fourier(u, consts, B, S):
    wc, m1, m2, N1 = consts
    N2 = FFT_N2
    ct = 1024
    W = N2 * 512
    u2 = u.reshape(B * N1, W)
    sds = jax.ShapeDtypeStruct((B * N1, W), BF16)
    ar, ai = pl.pallas_call(
        functools.partial(_fft1_kernel, N1=N1, ct=ct),
        grid=(B, W // ct),
        in_specs=[
            pl.BlockSpec((N1, ct), lambda b, j: (b, j)),
            pl.BlockSpec(wc.shape, lambda b, j: (0, 0)),
            pl.BlockSpec(m1.shape, lambda b, j: (0, 0)),
        ],
        out_specs=[pl.BlockSpec((N1, ct), lambda b, j: (b, j))] * 2,
        out_shape=[sds, sds],
        compiler_params=_cparams(("parallel", "parallel")),
        name="fft_stage1",
    )(u2, wc, m1)
    ar3 = ar.reshape(B * N1, N2, 512)
    ai3 = ai.reshape(B * N1, N2, 512)
    y = pl.pallas_call(
        _fft2_kernel,
        grid=(B, N1),
        in_specs=[
            pl.BlockSpec((1, N2, 512), lambda b, p: (b * N1 + p, 0, 0)),
            pl.BlockSpec((1, N2, 512), lambda b, p: (b * N1 + p, 0, 0)),
            pl.BlockSpec((1, N2, 2 * N2), lambda b, p: (p, 0, 0)),
        ],
        out_specs=pl.BlockSpec((N2, 512), lambda b, p: (b, p)),
        out_shape=jax.ShapeDtypeStruct((B * N2, N1 * 512), BF16),
        compiler_params=_cparams(("parallel", "parallel")),
        name="fft_stage2",
    )(ar3, ai3, m2)
    return y.reshape(B * S, 512)


def _merge_kernel(x_ref, oa_ref, ob_ref, oc_ref, od_ref, g1_ref, wg_ref, wb_ref, wo_ref, g2_ref, wr_ref,
                  x1_ref, h2_ref, aff_ref):
    x = x_ref[...]
    h = _rms(x, g1_ref[...]).astype(BF16)
    branches = (oa_ref, ob_ref, oc_ref, od_ref)
    mixed = None
    for n in range(N_BRANCHES):
        term = jax.nn.sigmoid(_mm(h, wg_ref[n])) * _mm(branches[n][...], wb_ref[n])
        mixed = term if mixed is None else mixed + term
    x1 = x + _mm(mixed.astype(BF16), wo_ref[...])
    x1_ref[...] = x1
    h2 = _rms(x1, g2_ref[...]).astype(BF16)
    h2_ref[...] = h2
    logits = _nt(wr_ref[...], h2)
    mx = jnp.max(logits, axis=0, keepdims=True)
    ex = jnp.exp(logits - mx)
    aff_ref[...] = ex / jnp.sum(ex, axis=0, keepdims=True)


def _merge(x2, oa, ob, oc, od, g1, wg, wb, wo, g2, wr_t):
    T = x2.shape[0]
    tm = TOKEN_TILE
    tok = lambda w: pl.BlockSpec((tm, w), lambda i: (i, 0))
    c2 = lambda i: (0, 0)
    c3 = lambda i: (0, 0, 0)
    return pl.pallas_call(
        _merge_kernel,
        grid=(T // tm,),
        in_specs=[
            tok(D_MODEL), tok(512), tok(512), tok(512), tok(512),
            pl.BlockSpec((1, D_MODEL), c2),
            pl.BlockSpec((N_BRANCHES, D_MODEL, D_MODEL), c3),
            pl.BlockSpec((N_BRANCHES, BRANCH_WIDTH, D_MODEL), c3),
            pl.BlockSpec((D_MODEL, D_MODEL), c2),
            pl.BlockSpec((1, D_MODEL), c2),
            pl.BlockSpec((N_EXPERTS, D_MODEL), c2),
        ],
        out_specs=[tok(D_MODEL), tok(D_MODEL), pl.BlockSpec((N_EXPERTS, tm), lambda i: (0, i))],
        out_shape=[jax.ShapeDtypeStruct((T, D_MODEL), F32), jax.ShapeDtypeStruct((T, D_MODEL), BF16),
                   jax.ShapeDtypeStruct((N_EXPERTS, T), F32)],
        compiler_params=_cparams(("parallel",)),
        name="merge",
    )(x2, oa, ob, oc, od, g1, wg, wb, wo, g2, wr_t)


def _cap_rows(T):
    cap = CAPACITY_FACTOR * T // N_EXPERTS
    ntiles = T // MOE_TILE
    need = cap + BF16_SUBLANES * ntiles + MOE_SUB
    return cap, ntiles, -(-need // FFN_ROWS) * FFN_ROWS


def _route_kernel(aff_ref, tri_ref, rank_ref, cnt_ref, off_ref, *, T, cap, ntiles):
    tk = MOE_TILE
    bits = pltpu.bitcast(aff_ref[...], jnp.int32)
    capf = jnp.float32(cap)

    def count(pred):
        return jnp.sum(jnp.where(pred, 1.0, 0.0), axis=-1, keepdims=True)

    def vbody(i, v):
        cand = v | lax.shift_left(jnp.int32(1), 30 - i)
        return jnp.where(count(bits >= cand) >= capf, cand, v)

    v = lax.fori_loop(0, 31, vbody, jnp.zeros((N_EXPERTS, 1), jnp.int32))
    need = capf - count(bits > v)
    ties = bits == v
    idx = lax.broadcasted_iota(jnp.int32, (N_EXPERTS, T), 1)
    nbits = max(1, (T - 1).bit_length())

    def jbody(i, j0):
        cand = j0 | lax.shift_left(jnp.int32(1), nbits - 1 - i)
        return jnp.where(count(ties & (idx < cand)) < need, cand, j0)

    j0 = lax.fori_loop(0, nbits, jbody, jnp.zeros((N_EXPERTS, 1), jnp.int32))
    sel = (bits > v) | (ties & (idx <= j0))
    tri = tri_ref[...]
    cnt_ref[...] = jnp.zeros_like(cnt_ref)
    for i in range(ntiles):
        sl = slice(i * tk, (i + 1) * tk)
        s_i = sel[:, sl]
        sf = jnp.where(s_i, 1.0, 0.0)
        r = _mm(sf.astype(BF16), tri)
        rank_ref[:, sl] = jnp.where(s_i, r, -1.0)
        cnt_ref[:, i:i + 1] = jnp.sum(sf, axis=-1, keepdims=True)
    cnt = cnt_ref[...]
    units = jnp.ceil(cnt * (1.0 / BF16_SUBLANES))
    off_ref[...] = (_mm(units.astype(BF16), tri[:LANES, :LANES]) * BF16_SUBLANES).astype(jnp.int32)


def _route(aff_t, tri, T):
    cap, ntiles, _ = _cap_rows(T)
    assert ntiles <= LANES and MOE_TILE // BF16_SUBLANES <= 256
    full = lambda s: pl.BlockSpec(s, lambda: tuple(0 for _ in s))
    rank, cnt, off = pl.pallas_call(
        functools.partial(_route_kernel, T=T, cap=cap, ntiles=ntiles),
        in_specs=[full((N_EXPERTS, T)), full(tri.shape)],
        out_specs=[full((N_EXPERTS, T)), full((N_EXPERTS, LANES)), full((N_EXPERTS, LANES))],
        out_shape=[jax.ShapeDtypeStruct((N_EXPERTS, T), F32), jax.ShapeDtypeStruct((N_EXPERTS, LANES), F32),
                   jax.ShapeDtypeStruct((N_EXPERTS, LANES), jnp.int32)],
        compiler_params=_cparams(()),
        name="moe_route",
    )(aff_t, tri)
    return rank, cnt.astype(jnp.int32), off


def _gather_kernel(cnt_ref, off_ref, rank_ref, aff_ref, h_ref, xe_ref, gate_ref):
    e, i = pl.program_id(0), pl.program_id(1)
    R = MOE_SUB

    @pl.when(i == 0)
    def _():
        xe_ref[...] = jnp.zeros_like(xe_ref)
        gate_ref[...] = jnp.zeros_like(gate_ref)

    c = cnt_ref[e, i]
    off = off_ref[e, i]
    rank = rank_ref[0]
    aff = aff_ref[0]
    jrow = lax.broadcasted_iota(jnp.int32, (R, MOE_TILE), 0).astype(F32)

    def body(r, carry):
        hit = rank == jrow + (r * R).astype(F32)
        x = _mm(jnp.where(hit, 1.0, 0.0).astype(BF16), h_ref[...])
        g = jnp.sum(jnp.where(hit, aff, 0.0), axis=-1, keepdims=True)
        r0 = pl.multiple_of(off + r * R, BF16_SUBLANES)
        xe_ref[0, pl.ds(r0, R), :] = x.astype(BF16)
        gate_ref[0, pl.ds(r0, R), :] = jnp.broadcast_to(g, (R, LANES))
        return carry

    lax.fori_loop(0, (c + R - 1) // R, body, 0)


def _gather(cnt, off, rank3, aff3, h2, T):
    _, ntiles, rows = _cap_rows(T)
    tk = MOE_TILE
    return pl.pallas_call(
        _gather_kernel,
        grid_spec=pltpu.PrefetchScalarGridSpec(
            num_scalar_prefetch=2,
            grid=(N_EXPERTS, ntiles),
            in_specs=[
                pl.BlockSpec((1, 1, tk), lambda e, i, c, o: (e, 0, i)),
                pl.BlockSpec((1, 1, tk), lambda e, i, c, o: (e, 0, i)),
                pl.BlockSpec((tk, D_MODEL), lambda e, i, c, o: (i, 0)),
            ],
            out_specs=[
                pl.BlockSpec((1, rows, D_MODEL), lambda e, i, c, o: (e, 0, 0)),
                pl.BlockSpec((1, rows, LANES), lambda e, i, c, o: (e, 0, 0)),
            ],
        ),
        out_shape=[jax.ShapeDtypeStruct((N_EXPERTS, rows, D_MODEL), BF16),
                   jax.ShapeDtypeStruct((N_EXPERTS, rows, LANES), F32)],
        compiler_params=_cparams(("parallel", "arbitrary")),
        name="moe_gather",
    )(cnt, off, rank3, aff3, h2)


def _ffn_kernel(x_ref, gate_ref, wg_ref, wu_ref, wd_ref, y_ref):
    x = x_ref[0]
    a = _mm(x, wg_ref[0])
    u = _mm(x, wu_ref[0])
    hmid = (a * jax.nn.sigmoid(a) * u).astype(BF16)
    y_ref[0] = (_mm(hmid, wd_ref[0]) * gate_ref[0][:, :1]).astype(BF16)


def _ffn(xe, gate, wg, wu, wd):
    E, rows, _ = xe.shape
    wspec = lambda s: pl.BlockSpec((1,) + s, lambda e, j: (e, 0, 0))
    return pl.pallas_call(
        _ffn_kernel,
        grid=(E, rows // FFN_ROWS),
        in_specs=[
            pl.BlockSpec((1, FFN_ROWS, D_MODEL), lambda e, j: (e, j, 0)),
            pl.BlockSpec((1, FFN_ROWS, LANES), lambda e, j: (e, j, 0)),
            wspec((D_MODEL, EXPERT_FF)), wspec((D_MODEL, EXPERT_FF)), wspec((EXPERT_FF, D_MODEL)),
        ],
        out_specs=pl.BlockSpec((1, FFN_ROWS, D_MODEL), lambda e, j: (e, j, 0)),
        out_shape=jax.ShapeDtypeStruct(xe.shape, BF16),
        compiler_params=_cparams(("parallel", "parallel")),
        name="moe_ffn",
    )(xe, gate, wg, wu, wd)


def _scatter_kernel(cnt_ref, off_ref, x_ref, rank_ref, ye_ref, o_ref, seg_ref, oh_ref, extra_ref, sem, xsem):
    i = pl.program_id(0)
    R = MOE_SUB
    tk = MOE_TILE

    def seg_copy(e):
        r0 = pl.multiple_of(off_ref[e, i], BF16_SUBLANES)
        return pltpu.make_async_copy(ye_ref.at[e, pl.ds(r0, R)], seg_ref.at[pl.ds(e * R, R)], sem.at[e])

    for e in range(N_EXPERTS):
        seg_copy(e).start()
    jlane = lax.broadcasted_iota(jnp.int32, (tk, R), 1).astype(F32)
    for e in range(N_EXPERTS):
        oh_ref[:, e * R:(e + 1) * R] = jnp.where(rank_ref[:, e:e + 1] == jlane, 1.0, 0.0).astype(BF16)
    for e in range(N_EXPERTS):
        seg_copy(e).wait()
    o_ref[...] = x_ref[...] + _mm(oh_ref[...], seg_ref[...])

    for e in range(N_EXPERTS):
        c = cnt_ref[e, i]

        def body(r, carry, e=e):
            r0 = pl.multiple_of(off_ref[e, i] + r * R, BF16_SUBLANES)
            cp = pltpu.make_async_copy(ye_ref.at[e, pl.ds(r0, R)], extra_ref, xsem.at[0])
            cp.start()
            hit = rank_ref[:, e:e + 1] == jlane + (r * R).astype(F32)
            cp.wait()
            o_ref[...] += _mm(jnp.where(hit, 1.0, 0.0).astype(BF16), extra_ref[...])
            return carry

        lax.fori_loop(1, (c + R - 1) // R, body, 0)


def _scatter(cnt, off, x1, rank_t, ye, T):
    _, ntiles, rows = _cap_rows(T)
    tk = MOE_TILE
    R = MOE_SUB
    return pl.pallas_call(
        _scatter_kernel,
        grid_spec=pltpu.PrefetchScalarGridSpec(
            num_scalar_prefetch=2,
            grid=(ntiles,),
            in_specs=[
                pl.BlockSpec((tk, D_MODEL), lambda i, c, o: (i, 0)),
                pl.BlockSpec((tk, N_EXPERTS), lambda i, c, o: (i, 0)),
                pl.BlockSpec(memory_space=pl.ANY),
            ],
            out_specs=pl.BlockSpec((tk, D_MODEL), lambda i, c, o: (i, 0)),
            scratch_shapes=[
                pltpu.VMEM((N_EXPERTS * R, D_MODEL), BF16),
                pltpu.VMEM((tk, N_EXPERTS * R), BF16),
                pltpu.VMEM((R, D_MODEL), BF16),
                pltpu.SemaphoreType.DMA((N_EXPERTS,)),
                pltpu.SemaphoreType.DMA((1,)),
            ],
        ),
        out_shape=jax.ShapeDtypeStruct((T, D_MODEL), F32),
        compiler_params=_cparams(("arbitrary",)),
        name="moe_scatter",
    )(cnt, off, x1, rank_t, ye)


def _expert_choice(x1, h2, aff_t, tri, wg, wu, wd):
    T = x1.shape[0]
    rank, cnt, off = _route(aff_t, tri, T)
    xe, gate = _gather(cnt, off, rank.reshape(N_EXPERTS, 1, T), aff_t.reshape(N_EXPERTS, 1, T), h2, T)
    ye = _ffn(xe, gate, wg, wu, wd)
    return _scatter(cnt, off, x1, rank.T, ye, T)


def _pad_heads(w, n_heads, width, pad_to):
    k = w.shape[0]
    w = w.reshape(k, n_heads, width)
    return jnp.pad(w, ((0, 0), (0, 0), (0, pad_to - width))).reshape(k, n_heads * pad_to)


def _pack_layer(l, p):
    w_in = p['w_in'][l]
    edges = np.cumsum([0, 512, 128, 128, B_Q_LORA, B_KV_LORA + B_ROPE, 512, 512, 512, 512, 512, 512])
    a_q, a_k, a_v, b_cq, b_ckv, c_q, c_ff, c_fb, c_i, c_g, d_u = [w_in[:, edges[j]:edges[j + 1]] for j in range(11)]
    zeros = lambda n: jnp.zeros((D_MODEL, n), w_in.dtype)
    b_ckv_p = jnp.concatenate([b_ckv[:, :B_KV_LORA], zeros(B_NOPE), b_ckv[:, B_KV_LORA:], zeros(PAD_HEAD - B_QK)], axis=1)
    w_main = jnp.concatenate([a_q, c_q, c_ff, c_fb, c_i, c_g, a_k, a_v, b_cq, b_ckv_p], axis=1).astype(BF16)
    kv_up = p['b_kv_up'][l].reshape(B_KV_LORA, B_HEADS, B_NOPE + B_V)
    pad_g = lambda g: jnp.pad(g, (0, PAD_HEAD - B_QK)).reshape(1, PAD_HEAD)
    sink = p['a_sink'][l].astype(F32).reshape(A_KV_HEADS, A_HEADS // A_KV_HEADS)
    return dict(
        g_mix=p['norm_mix_g'][l].reshape(1, D_MODEL), w_main=w_main, w_u=d_u.astype(BF16),
        a_gq=p['a_q_norm_g'][l].reshape(1, A_HEAD_DIM), a_gk=p['a_k_norm_g'][l].reshape(1, A_HEAD_DIM),
        a_sink=jnp.repeat(sink, WINDOW, axis=1)[:, :, None],
        b_gqa=p['b_q_a_norm_g'][l].reshape(1, B_Q_LORA), b_gkva=p['b_kv_a_norm_g'][l].reshape(1, B_KV_LORA),
        b_wq=_pad_heads(p['b_q_up'][l], B_HEADS, B_QK, PAD_HEAD).astype(BF16),
        b_wk=_pad_heads(kv_up[:, :, :B_NOPE].reshape(B_KV_LORA, -1), B_HEADS, B_NOPE, PAD_HEAD).astype(BF16),
        b_wv=_pad_heads(kv_up[:, :, B_NOPE:].reshape(B_KV_LORA, -1), B_HEADS, B_V, PAD_HEAD).astype(BF16),
        b_gq=pad_g(p['b_q_norm_g'][l]), b_gk=pad_g(p['b_k_norm_g'][l]),
        c_gn=p['c_out_norm_g'][l].reshape(1, C_DV),
        w_gate=p['w_gate'][l].astype(BF16), w_branch=p['w_branch'][l].astype(BF16), w_out=p['w_out'][l].astype(BF16),
        g_ffn=p['norm_ffn_g'][l].reshape(1, D_MODEL), w_router_t=p['w_router'][l].T.astype(BF16),
        w_e_gate=p['w_e_gate'][l].astype(BF16), w_e_up=p['w_e_up'][l].astype(BF16), w_e_down=p['w_e_down'][l].astype(BF16),
    )


def _layer(x2, B, S, lp, lb, cst):
    proj, u = _in_proj(x2, lp['g_mix'], lp['w_main'], lp['w_u'])
    o_a = _win_attn(proj, lp['a_gq'], lp['a_gk'], lp['a_sink'], B, S)
    q, k, v = _mla_prep(proj, lp['b_gqa'], lp['b_gkva'], lp['b_wq'], lp['b_wk'], lp['b_wv'], lp['b_gq'], lp['b_gk'],
                        cst['rope'][S], S)
    o_b = _flash(q, k, v, B, S)
    o_f = _hgrn(proj, lb[0:1], cst['hgrn_f'], B, S, False)
    o_c = _hgrn(proj, lb[1:2], cst['hgrn_b'], B, S, True, o_f, lp['c_gn'])
    o_d = _fourier(u, cst['fft'][S], B, S)
    x1, h2, aff_t = _merge(x2, o_a, o_b, o_c, o_d, lp['g_mix'], lp['w_gate'], lp['w_branch'], lp['w_out'],
                           lp['g_ffn'], lp['w_router_t'])
    return _expert_choice(x1, h2, aff_t, cst['tri'], lp['w_e_gate'], lp['w_e_up'], lp['w_e_down'])


def kernel(x_prompt, x_sample, norm_mix_g, w_in, a_q_norm_g, a_k_norm_g, a_sink, b_q_a_norm_g, b_q_up,
           b_kv_a_norm_g, b_kv_up, b_q_norm_g, b_k_norm_g, c_lb_logits, c_out_norm_g, w_branch, w_gate, w_out,
           norm_ffn_g, w_router, w_e_gate, w_e_up, w_e_down):
    p = dict(norm_mix_g=norm_mix_g, w_in=w_in, a_q_norm_g=a_q_norm_g, a_k_norm_g=a_k_norm_g, a_sink=a_sink,
             b_q_a_norm_g=b_q_a_norm_g, b_q_up=b_q_up, b_kv_a_norm_g=b_kv_a_norm_g, b_kv_up=b_kv_up,
             b_q_norm_g=b_q_norm_g, b_k_norm_g=b_k_norm_g, c_out_norm_g=c_out_norm_g, w_branch=w_branch,
             w_gate=w_gate, w_out=w_out, norm_ffn_g=norm_ffn_g, w_router=w_router, w_e_gate=w_e_gate,
             w_e_up=w_e_up, w_e_down=w_e_down)
    lb_w = jax.nn.softmax(c_lb_logits.astype(F32), axis=0)
    lb_all = jnp.cumsum(lb_w, axis=0) - lb_w[0:1]
    groups = [x_prompt, x_sample]
    seqs = sorted({g.shape[1] for g in groups})
    tk = MOE_TILE
    cst = dict(
        rope={S: _rope_tables(S) for S in seqs},
        fft={S: _fft_consts(S) for S in seqs},
        hgrn_f=_hgrn_consts(HGRN_CHUNK, False), hgrn_b=_hgrn_consts(HGRN_CHUNK, True),
        tri=jnp.asarray(np.triu(np.ones((tk, tk), np.float32), 1), BF16),
    )
    layers = [_pack_layer(l, p) for l in range(DEPTH)]
    outs = []
    for xg in groups:
        B, S, _ = xg.shape
        y = xg.reshape(B * S, D_MODEL)
        for l in range(DEPTH):
            y = _layer(y, B, S, layers[l], lb_all[l], cst)
        outs.append(y.reshape(B, S, D_MODEL))
    return tuple(outs)
```

```python
import functools
import math

import numpy as np
import jax
import jax.numpy as jnp
from jax import lax
from jax.experimental import pallas as pl
from jax.experimental.pallas import tpu as pltpu

F32 = jnp.float32
BF16 = jnp.bfloat16

D_MODEL = 1024
DEPTH = 2
A_HEADS, A_KV_HEADS, A_HEAD_DIM, WINDOW = 8, 2, 64, 128
B_HEADS, B_Q_LORA, B_KV_LORA, B_NOPE, B_ROPE, B_V = 8, 256, 128, 64, 32, 64
B_QK = B_NOPE + B_ROPE
ROPE_THETA = 10000.0
C_HEADS, C_DK, C_DV = 4, 128, 128
D_GROUPS, D_GROUP_DIM = 4, 128
N_BRANCHES, BRANCH_WIDTH = 4, 512
N_EXPERTS, EXPERT_FF, CAPACITY_FACTOR = 16, 1024, 2
EPS = 1e-6
MASK_VALUE = -1e30
TINY = 1e-30
LOG2E = 1.4426950408889634

SCORE_LANE = B_QK
SUM_LANE = B_V
FAST_SOFTMAX_MAX_BOUND = 30.0

LANES = 128
BF16_SUBLANES = 16
V7X_VMEM_LIMIT = 56 * 1024 * 1024

TOKEN_TILE = 512
PAD_HEAD = 128
HGRN_CHUNK = 128
FFT_N2 = 128
FFT1_COLS = 4096
FFT2_P = 4
FLASH_TQ, FLASH_TK, FLASH_UNROLL = 256, 512, 3
MOE_TILE = 512
MOE_SUB = 128
FFN_ROWS = 512

OFF_AQ, OFF_CQ, OFF_CFF, OFF_CFB, OFF_CI, OFF_CG = 0, 512, 1024, 1536, 2048, 2560
OFF_AK, OFF_AV, OFF_BCQ, OFF_BLAT, OFF_BKPE, OFF_BKPESW = 3072, 3200, 3328, 3584, 3712, 3840
PROJ_W = 3968


def _cparams(sem, vmem=V7X_VMEM_LIMIT):
    return pltpu.CompilerParams(dimension_semantics=sem, vmem_limit_bytes=vmem)


def _nt(a, b):
    return lax.dot_general(a, b, (((1,), (1,)), ((), ())), preferred_element_type=F32)


def _mm(a, b):
    return jnp.dot(a, b, preferred_element_type=F32)


def _rms(x, g):
    return x * lax.rsqrt(jnp.mean(x * x, axis=-1, keepdims=True) + EPS) * g


def _in_proj_kernel(x_ref, g_ref, w_ref, wu_ref, p_ref, u_ref):
    h = _rms(x_ref[...], g_ref[...]).astype(BF16)
    for c0 in range(0, PROJ_W, 512):
        c1 = min(c0 + 512, PROJ_W)
        p_ref[:, c0:c1] = _mm(h, w_ref[:, c0:c1]).astype(BF16)
    u_ref[...] = _mm(h, wu_ref[...]).astype(BF16)


def _in_proj(x2, g, w_main, w_u):
    T = x2.shape[0]
    tm = TOKEN_TILE
    return pl.pallas_call(
        _in_proj_kernel,
        grid=(T // tm,),
        in_specs=[
            pl.BlockSpec((tm, D_MODEL), lambda i: (i, 0)),
            pl.BlockSpec((1, D_MODEL), lambda i: (0, 0)),
            pl.BlockSpec((D_MODEL, PROJ_W), lambda i: (0, 0)),
            pl.BlockSpec((D_MODEL, 512), lambda i: (0, 0)),
        ],
        out_specs=[
            pl.BlockSpec((tm, PROJ_W), lambda i: (i, 0)),
            pl.BlockSpec((tm, 512), lambda i: (i, 0)),
        ],
        out_shape=[jax.ShapeDtypeStruct((T, PROJ_W), BF16), jax.ShapeDtypeStruct((T, 512), BF16)],
        compiler_params=_cparams(("parallel",)),
        name="in_proj",
    )(x2, g, w_main, w_u)


def _win_attn_kernel(q_ref, kp_ref, kc_ref, kn_ref, vp_ref, vc_ref, vn_ref, gq_ref, gk_ref, sink_ref, o_ref, *, nb):
    n = pl.program_id(1)
    W = WINDOW
    G = A_HEADS // A_KV_HEADS
    q = q_ref[...].astype(F32)
    kcat = jnp.concatenate([kp_ref[...], kc_ref[...], kn_ref[...]], axis=0).astype(F32)
    vcat = jnp.concatenate([vp_ref[...], vc_ref[...], vn_ref[...]], axis=0)
    gq = gq_ref[...]
    gk = gk_ref[...]
    row = lax.broadcasted_iota(jnp.int32, (G * W, 3 * W), 0)
    col = lax.broadcasted_iota(jnp.int32, (G * W, 3 * W), 1)
    qi = row % W
    dist = jnp.abs(col - W - qi)
    valid = dist <= W
    valid = valid & ((col >= W) | (n > 0)) & ((col < 2 * W) | (n < nb - 1))
    distf = dist.astype(F32)
    hrow = lax.broadcasted_iota(jnp.int32, (G * W, 1), 0) // W
    for g in range(A_KV_HEADS):
        qs = []
        for j in range(G):
            h = g * G + j
            qs.append(_rms(q[:, h * A_HEAD_DIM:(h + 1) * A_HEAD_DIM], gq))
        qg = (jnp.concatenate(qs, axis=0) * (A_HEAD_DIM ** -0.5)).astype(BF16)
        kg = _rms(kcat[:, g * A_HEAD_DIM:(g + 1) * A_HEAD_DIM], gk).astype(BF16)
        vg = vcat[:, g * A_HEAD_DIM:(g + 1) * A_HEAD_DIM]
        slope = jnp.exp2(-(hrow + (g * G + 1)).astype(F32))
        s = _nt(qg, kg) - slope * distf
        s = jnp.where(valid, s, MASK_VALUE)
        sk = sink_ref[g]
        m = jnp.maximum(jnp.max(s, axis=-1, keepdims=True), sk)
        p = jnp.exp(s - m)
        den = jnp.sum(p, axis=-1, keepdims=True) + jnp.exp(sk - m)
        o = _mm(p.astype(BF16), vg) / den
        for j in range(G):
            h = g * G + j
            o_ref[:, h * A_HEAD_DIM:(h + 1) * A_HEAD_DIM] = o[j * W:(j + 1) * W].astype(BF16)


def _win_fast_consts():
    d, G = A_HEAD_DIM, A_HEADS // A_KV_HEADS
    qw = A_HEADS * d
    lane = np.arange(qw)
    bd = (lane[:, None] // d == lane[None, :] // d).astype(np.float32) / d
    l128 = np.arange(LANES)
    dup = np.stack([(l128[:, None] == g * d + l128[None, :] % d) for g in range(A_KV_HEADS)]).astype(np.float32)
    sel = np.stack([np.stack([(l128[:, None] == g * d + l128[None, :] - par * d) & (l128[None, :] // d == par)
                              for par in range(2)]) for g in range(A_KV_HEADS)]).astype(np.float32)
    return jnp.asarray(bd, BF16), jnp.asarray(dup, BF16), jnp.asarray(sel, BF16)


def _win_fast_tables(gq, gk, sink):
    W = WINDOW
    bound = math.sqrt(A_HEAD_DIM) * jnp.max(jnp.abs(gq)) * jnp.max(jnp.abs(gk))
    sink = sink.astype(F32)
    m = jnp.maximum(bound, sink)
    slopes = 2.0 ** (-8.0 * jnp.arange(1, A_HEADS + 1, dtype=F32) / A_HEADS)
    r = jnp.arange(W)[:, None]
    c = jnp.arange(3 * W)[None, :]
    dist = jnp.abs(c - W - r)
    inside = dist <= W
    variants = []
    for first, last in ((False, False), (True, False), (False, True), (True, True)):
        valid = inside & ((c >= W) | (not first)) & ((c < 2 * W) | (not last))
        b = LOG2E * (-slopes[:, None, None] * dist[None].astype(F32) - m[:, None, None])
        variants.append(jnp.where(valid[None], b, MASK_VALUE))
    sink_term = jnp.exp2(LOG2E * (sink - m))
    return bound, jnp.stack(variants), jnp.broadcast_to(sink_term[:, None, None], (A_HEADS, 1, LANES))


def _win_fast_kernel(q_ref, kp_ref, kc_ref, kn_ref, vp_ref, vc_ref, vn_ref, gq_ref, gk_ref, bd_ref, dup_ref, sel_ref,
                     bias_ref, st_ref, o_ref):
    d = A_HEAD_DIM
    q = q_ref[...].astype(F32)
    bd = bd_ref[...]
    qf = q * lax.rsqrt(_mm((q * q).astype(BF16), bd) + EPS) * gq_ref[...]
    kcat = jnp.concatenate([kp_ref[...], kc_ref[...], kn_ref[...]], axis=0).astype(F32)
    kss = _mm((kcat * kcat).astype(BF16), bd[:LANES, :LANES])
    kn = (kcat * lax.rsqrt(kss + EPS) * gk_ref[...]).astype(BF16)
    vcat = jnp.concatenate([vp_ref[...], vc_ref[...], vn_ref[...]], axis=0)
    ones = jnp.ones((3 * WINDOW, LANES), BF16)
    lane = lax.broadcasted_iota(jnp.int32, (1, LANES), 1)
    half = [jnp.where(lane < d, 1.0, 0.0), jnp.where(lane >= d, 1.0, 0.0)]
    G = A_HEADS // A_KV_HEADS
    for g in range(A_KV_HEADS):
        kk = _mm(kn, dup_ref[g]).astype(BF16)
        rhs = [jnp.concatenate([_mm(vcat, sel_ref[g, par]).astype(BF16), ones], axis=1) for par in range(2)]
        for jj in range(G // 2):
            pair = g * (G // 2) + jj
            qp = qf[:, pair * LANES:(pair + 1) * LANES]
            acc = None
            for par in range(2):
                h = 2 * pair + par
                s = _nt((qp * half[par]).astype(BF16), kk) + bias_ref[0, h]
                res = _mm(jnp.exp2(s).astype(BF16), rhs[par])
                o = res[:, :LANES] / (res[:, LANES:] + st_ref[h])
                acc = o if acc is None else acc + o
            o_ref[:, pair * LANES:(pair + 1) * LANES] = acc.astype(BF16)


def _win_attn_fast(proj, gq_t, gk_t, consts, bias, sink_term, B, S):
    T = B * S
    W = WINDOW
    nb = S // W
    kcol, vcol = OFF_AK // 128, OFF_AV // 128
    bd, dup, sel = consts
    cur = lambda b, n: b * nb + n
    prev = lambda b, n: b * nb + jnp.maximum(n - 1, 0)
    nxt = lambda b, n: b * nb + jnp.minimum(n + 1, nb - 1)
    spec = lambda rowf, c: pl.BlockSpec((W, 128), lambda b, n: (rowf(b, n), c))
    variant = lambda b, n: ((n == 0).astype(jnp.int32) + 2 * (n == nb - 1).astype(jnp.int32), 0, 0, 0)
    c2 = lambda b, n: (0, 0)
    c3 = lambda b, n: (0, 0, 0)
    return pl.pallas_call(
        _win_fast_kernel,
        grid=(B, nb),
        in_specs=[
            pl.BlockSpec((W, 512), lambda b, n: (cur(b, n), OFF_AQ // 512)),
            spec(prev, kcol), spec(cur, kcol), spec(nxt, kcol),
            spec(prev, vcol), spec(cur, vcol), spec(nxt, vcol),
            pl.BlockSpec((1, 512), c2), pl.BlockSpec((1, LANES), c2),
            pl.BlockSpec(bd.shape, c2), pl.BlockSpec(dup.shape, c3), pl.BlockSpec(sel.shape, lambda b, n: (0, 0, 0, 0)),
            pl.BlockSpec((1, A_HEADS, W, 3 * W), variant),
            pl.BlockSpec((A_HEADS, 1, LANES), c3),
        ],
        out_specs=pl.BlockSpec((W, 512), lambda b, n: (cur(b, n), 0)),
        out_shape=jax.ShapeDtypeStruct((T, 512), BF16),
        compiler_params=_cparams(("parallel", "parallel")),
        name="win_attn_fast",
    )(proj, proj, proj, proj, proj, proj, proj, gq_t, gk_t, bd, dup, sel, bias, sink_term)


def _win_attn(proj, gq, gk, sink_cols, B, S):
    T = B * S
    W = WINDOW
    nb = S // W
    kcol, vcol = OFF_AK // 128, OFF_AV // 128

    def cur(b, n):
        return b * nb + n

    def prev(b, n):
        return b * nb + jnp.maximum(n - 1, 0)

    def nxt(b, n):
        return b * nb + jnp.minimum(n + 1, nb - 1)

    def spec(rowf, c):
        return pl.BlockSpec((W, 128), lambda b, n: (rowf(b, n), c))

    return pl.pallas_call(
        functools.partial(_win_attn_kernel, nb=nb),
        grid=(B, nb),
        in_specs=[
            pl.BlockSpec((W, 512), lambda b, n: (cur(b, n), OFF_AQ // 512)),
            spec(prev, kcol), spec(cur, kcol), spec(nxt, kcol),
            spec(prev, vcol), spec(cur, vcol), spec(nxt, vcol),
            pl.BlockSpec((1, A_HEAD_DIM), lambda b, n: (0, 0)),
            pl.BlockSpec((1, A_HEAD_DIM), lambda b, n: (0, 0)),
            pl.BlockSpec((A_KV_HEADS, 4 * W, 1), lambda b, n: (0, 0, 0)),
        ],
        out_specs=pl.BlockSpec((W, 512), lambda b, n: (cur(b, n), 0)),
        out_shape=jax.ShapeDtypeStruct((T, 512), BF16),
        compiler_params=_cparams(("parallel", "parallel")),
        name="win_attn",
    )(proj, proj, proj, proj, proj, proj, proj, gq, gk, sink_cols)


def _rope_tables(S):
    half = B_ROPE // 2
    inv = 1.0 / (ROPE_THETA ** (np.arange(half, dtype=np.float32) / half))
    ang = np.arange(S, dtype=np.float32)[:, None] * inv[None, :]
    cos, sin = np.cos(ang), np.sin(ang)
    cosf = np.ones((S, PAD_HEAD), np.float32)
    sinf = np.zeros((S, PAD_HEAD), np.float32)
    cosf[:, B_NOPE:B_NOPE + half] = cos
    cosf[:, B_NOPE + half:B_NOPE + 2 * half] = cos
    sinf[:, B_NOPE:B_NOPE + half] = -sin
    sinf[:, B_NOPE + half:B_NOPE + 2 * half] = sin
    return jnp.asarray(cosf), jnp.asarray(sinf)


def _swap_rope_halves(w):
    half = B_ROPE // 2
    z = jnp.zeros_like(w)
    z = z.at[..., B_NOPE:B_NOPE + half].set(w[..., B_NOPE + half:B_NOPE + 2 * half])
    return z.at[..., B_NOPE + half:B_NOPE + 2 * half].set(w[..., B_NOPE:B_NOPE + half])


def _mla_prep_kernel(cq_ref, lat_ref, kpe_ref, kpesw_ref, gqa_ref, gkva_ref, wq_ref, wqsw_ref, wk_ref, wv_ref,
                     gq_ref, gqsw_ref, gk_ref, gksw_ref, cos_ref, sin_ref, qadd_ref, kadd_ref, vadd_ref,
                     q_ref, k_ref, v_ref):
    cosf, sinf = cos_ref[...], sin_ref[...]
    qscale = LOG2E * B_QK ** -0.5
    g1q, g2q = gq_ref[...] * cosf * qscale, gqsw_ref[...] * sinf * qscale
    g1k, g2k = gk_ref[...] * cosf, gksw_ref[...] * sinf

    def inv_rms(x):
        return lax.rsqrt(jnp.sum(x * x, axis=-1, keepdims=True) * (1.0 / B_QK) + EPS)

    cq = _rms(cq_ref[...].astype(F32), gqa_ref[...]).astype(BF16)
    qall = _mm(cq, wq_ref[...])
    qsw = _mm(cq, wqsw_ref[...])
    c = _rms(lat_ref[...].astype(F32), gkva_ref[...]).astype(BF16)
    kall = _mm(c, wk_ref[...])
    v_ref[...] = (_mm(c, wv_ref[...]) + vadd_ref[...]).astype(BF16)
    kpe = kpe_ref[...].astype(F32)
    krot = kpesw_ref[...].astype(F32) * g2k
    qadd, kadd = qadd_ref[...], kadd_ref[...]
    for h in range(B_HEADS):
        sl = slice(h * PAD_HEAD, (h + 1) * PAD_HEAD)
        x = qall[:, sl]
        q_ref[:, sl] = ((x * g1q + qsw[:, sl] * g2q) * inv_rms(x) + qadd).astype(BF16)
        y = kall[:, sl] + kpe
        k_ref[:, sl] = ((y * g1k + krot) * inv_rms(y) + kadd).astype(BF16)


def _mla_prep(proj, lp, tables, S):
    T = proj.shape[0]
    tm = TOKEN_TILE
    nper = S // tm
    HW = B_HEADS * PAD_HEAD
    cst = lambda i: (0, 0)
    tab = pl.BlockSpec((tm, PAD_HEAD), lambda i: (i % nper, 0))
    row = pl.BlockSpec((1, PAD_HEAD), cst)
    col = lambda off: pl.BlockSpec((tm, PAD_HEAD), lambda i: (i, off // PAD_HEAD))
    out = jax.ShapeDtypeStruct((T, HW), BF16)
    return pl.pallas_call(
        _mla_prep_kernel,
        grid=(T // tm,),
        in_specs=[
            pl.BlockSpec((tm, 256), lambda i: (i, OFF_BCQ // 256)),
            col(OFF_BLAT), col(OFF_BKPE), col(OFF_BKPESW),
            pl.BlockSpec((1, B_Q_LORA), cst), pl.BlockSpec((1, B_KV_LORA), cst),
            pl.BlockSpec((B_Q_LORA, HW), cst), pl.BlockSpec((B_Q_LORA, HW), cst),
            pl.BlockSpec((B_KV_LORA, HW), cst), pl.BlockSpec((B_KV_LORA, HW), cst),
            row, row, row, row, tab, tab, row, row, pl.BlockSpec((1, HW), cst),
        ],
        out_specs=[pl.BlockSpec((tm, HW), lambda i: (i, 0))] * 3,
        out_shape=[out, out, out],
        compiler_params=_cparams(("parallel",)),
        name="mla_prep",
    )(proj, proj, proj, proj, lp['b_gqa'], lp['b_gkva'], lp['b_wq'], lp['b_wqsw'], lp['b_wk'], lp['b_wv'],
      lp['b_gq'], lp['b_gqsw'], lp['b_gk'], lp['b_gksw'], *tables, *lp['b_adds'])


def _flash_kernel(q_ref, k_ref, v_ref, o_ref, *, S, online):
    tq, tk = FLASH_TQ, FLASH_TK
    nk = S // tk
    heads = [slice(hh * PAD_HEAD, (hh + 1) * PAD_HEAD) for hh in range(2)]
    qs = [q_ref[:, sl] for sl in heads]
    a0 = jnp.zeros((tq, PAD_HEAD), F32)

    def probs(j, hh):
        r0 = pl.multiple_of(j * tk, tk)
        return jnp.exp2(_nt(qs[hh], k_ref[pl.ds(r0, tk), heads[hh]])).astype(BF16)

    def pv(j, hh, p):
        r0 = pl.multiple_of(j * tk, tk)
        return _mm(p, v_ref[pl.ds(r0, tk), heads[hh]])

    if online:
        def body(j, carry):
            r0 = pl.multiple_of(j * tk, tk)
            out = []
            for hh, sl in enumerate(heads):
                s = _nt(qs[hh], k_ref[pl.ds(r0, tk), sl])
                m, acc = carry[hh]
                mn = jnp.maximum(m, jnp.max(s, axis=-1, keepdims=True))
                out.append((mn, acc * jnp.exp2(m - mn) + _mm(jnp.exp2(s - mn).astype(BF16), v_ref[pl.ds(r0, tk), sl])))
            return tuple(out)

        init = (jnp.full((tq, 1), MASK_VALUE, F32), a0)
        res = lax.fori_loop(0, nk, body, (init, init))
        accs = [res[0][1], res[1][1]]
    else:
        def body(j, carry):
            out = []
            for hh in range(2):
                acc, p_prev = carry[hh]
                out.append((acc + pv(j - 1, hh, p_prev), probs(j, hh)))
            return tuple(out)

        res = lax.fori_loop(1, nk, body, ((a0, probs(0, 0)), (a0, probs(0, 1))), unroll=FLASH_UNROLL)
        accs = [res[hh][0] + pv(nk - 1, hh, res[hh][1]) for hh in range(2)]
    for hh in range(2):
        acc = accs[hh]
        o_ref[:, hh * B_V:(hh + 1) * B_V] = (acc[:, :B_V] / acc[:, SUM_LANE:SUM_LANE + 1]).astype(BF16)


def _flash(q, k, v, B, S, online):
    T = B * S
    tq = FLASH_TQ
    nq = S // tq
    return pl.pallas_call(
        functools.partial(_flash_kernel, S=S, online=online),
        grid=(B, B_HEADS // 2, nq),
        in_specs=[
            pl.BlockSpec((tq, 2 * PAD_HEAD), lambda b, h, i: (b * nq + i, h)),
            pl.BlockSpec((S, 2 * PAD_HEAD), lambda b, h, i: (b, h)),
            pl.BlockSpec((S, 2 * PAD_HEAD), lambda b, h, i: (b, h)),
        ],
        out_specs=pl.BlockSpec((tq, 2 * B_V), lambda b, h, i: (b * nq + i, h)),
        out_shape=jax.ShapeDtypeStruct((T, B_HEADS * B_V), BF16),
        compiler_params=_cparams(("parallel", "parallel", "parallel")),
        name="mla_flash_online" if online else "mla_flash",
    )(q, k, v)


def _hgrn_consts(L, reverse):
    levels = []
    m = 1
    while m < L:
        levels.append(m)
        m *= 2
    nl = len(levels)
    t = np.arange(L)
    wall = np.zeros((nl + 2, L, L), np.float32)
    right = np.zeros((nl, L, 1), np.float32)
    mask = np.zeros((nl + 1, L, L), np.float32)
    for li, m in enumerate(levels):
        blk = t // (2 * m)
        mid = blk * 2 * m + m
        isr = t >= mid
        for tt in range(L):
            if isr[tt]:
                wall[li, tt, mid[tt]:tt + 1] = 1.0
            else:
                wall[li, tt, tt + 1:mid[tt]] = 1.0
        right[li, :, 0] = isr
        mask[li] = (blk[:, None] == blk[None, :]) & isr[:, None] & (~isr[None, :])
    mask[nl] = np.eye(L)
    wall[nl] = np.tril(np.ones((L, L)))
    wall[nl + 1] = np.triu(np.ones((L, L)), 1)
    if reverse:
        wall = wall[:, ::-1, ::-1]
        right = right[:, ::-1]
        mask = mask[:, ::-1, ::-1]
    return (jnp.asarray(np.ascontiguousarray(wall.reshape((nl + 2) * L, L)), BF16),
            jnp.asarray(np.ascontiguousarray(right)), jnp.asarray(np.ascontiguousarray(mask)), nl)


def _hgrn_kernel(*refs, L, nl, reverse):
    if reverse:
        q_ref, z_ref, v_ref, lb_ref, wall_ref, right_ref, mask_ref, of_ref, g_ref, gn_ref, o_ref, st_ref = refs
    else:
        q_ref, z_ref, v_ref, lb_ref, wall_ref, right_ref, mask_ref, o_ref, st_ref = refs

    @pl.when(pl.program_id(1) == 0)
    def _():
        st_ref[...] = jnp.zeros_like(st_ref)

    lb = lb_ref[...]
    z = z_ref[...].astype(F32)
    sig = jax.nn.sigmoid(z)
    lf = jnp.log(jnp.maximum(lb + (1.0 - lb) * sig, TINY))
    key = (1.0 - lb) * (1.0 - sig)
    qx = q_ref[...].astype(F32)
    qh = qx * jax.nn.sigmoid(qx)
    hi = lf.astype(BF16)
    lo = (lf - hi.astype(F32)).astype(BF16)
    wall = wall_ref[...]
    e_all = jnp.exp(_mm(wall, hi) + _mm(wall, lo))
    last = 0 if reverse else L - 1
    for h in range(C_HEADS):
        sl = slice(h * C_DK, (h + 1) * C_DK)
        qh_h, key_h, v_h = qh[:, sl], key[:, sl], v_ref[:, sl]
        scores = _nt(qh_h.astype(BF16), key_h.astype(BF16)) * mask_ref[nl]
        for li in range(nl):
            x = (jnp.where(right_ref[li] > 0.0, qh_h, key_h) * e_all[li * L:(li + 1) * L, sl]).astype(BF16)
            scores = scores + _nt(x, x) * mask_ref[li]
        o = _mm(scores.astype(BF16), v_h)
        ep = e_all[nl * L:(nl + 1) * L, sl]
        er = e_all[(nl + 1) * L:(nl + 2) * L, sl]
        st = st_ref[h]
        o = o + _nt((qh_h * ep).astype(BF16), st.astype(BF16))
        st_ref[h] = st * ep[last:last + 1] + _mm(v_h.T, (key_h * er).astype(BF16))
        if reverse:
            tot = o + of_ref[:, sl]
            o_ref[:, sl] = (_rms(tot, gn_ref[...]) * (lambda u: u * jax.nn.sigmoid(u))(g_ref[:, sl].astype(F32))).astype(BF16)
        else:
            o_ref[:, sl] = o


def _hgrn(proj, lb_dir, consts, B, S, reverse, o_fwd=None, gn=None):
    T = B * S
    L = HGRN_CHUNK
    nc = S // L
    wall, right, mask, nl = consts

    def rows(b, c):
        return b * nc + (nc - 1 - c if reverse else c)

    def pspec(off):
        return pl.BlockSpec((L, 512), lambda b, c: (rows(b, c), off // 512))

    cst2 = lambda b, c: (0, 0)
    cst3 = lambda b, c: (0, 0, 0)
    in_specs = [
        pspec(OFF_CQ), pspec(OFF_CFB if reverse else OFF_CFF), pspec(OFF_CI),
        pl.BlockSpec((1, 512), cst2),
        pl.BlockSpec(wall.shape, cst2), pl.BlockSpec(right.shape, cst3), pl.BlockSpec(mask.shape, cst3),
    ]
    args = [proj, proj, proj, lb_dir, wall, right, mask]
    if reverse:
        in_specs += [pl.BlockSpec((L, 512), lambda b, c: (rows(b, c), 0)), pspec(OFF_CG), pl.BlockSpec((1, C_DV), cst2)]
        args += [o_fwd, proj, gn]
    return pl.pallas_call(
        functools.partial(_hgrn_kernel, L=L, nl=nl, reverse=reverse),
        grid=(B, nc),
        in_specs=in_specs,
        out_specs=pl.BlockSpec((L, 512), lambda b, c: (rows(b, c), 0)),
        out_shape=jax.ShapeDtypeStruct((T, 512), BF16 if reverse else F32),
        scratch_shapes=[pltpu.VMEM((C_HEADS, C_DV, C_DK), F32)],
        compiler_params=_cparams(("parallel", "arbitrary")),
        name="hgrn_bwd" if reverse else "hgrn_fwd",
    )(*args)


def _fft_consts(S):
    N2 = FFT_N2
    N1 = S // N2
    d = np.arange(D_GROUP_DIM)
    ang = 2.0 * np.pi * ((d[:, None] * d[None, :]) % D_GROUP_DIM) / D_GROUP_DIM
    wc = np.concatenate([np.cos(ang), -np.sin(ang)], axis=1) / math.sqrt(D_GROUP_DIM)
    a = np.arange(N1)
    ang1 = 2.0 * np.pi * ((a[:, None] * a[None, :]) % N1) / N1
    c1, s1 = np.cos(ang1), np.sin(ang1)
    m1 = np.block([[c1, s1], [-s1, c1]])
    p1 = jnp.arange(N1, dtype=jnp.int32)[:, None, None]
    p2 = jnp.arange(N2, dtype=jnp.int32)[None, :, None]
    s2 = jnp.arange(N2, dtype=jnp.int32)[None, None, :]
    kk = (s2 * (p1 + N1 * p2)) % S
    th = kk.astype(F32) * (2.0 * math.pi / S)
    m2 = jnp.concatenate([jnp.cos(th), jnp.sin(th)], axis=2) * (1.0 / math.sqrt(S))
    return jnp.asarray(wc, BF16), jnp.asarray(m1, BF16), m2.astype(BF16), N1


def _fft1_kernel(u_ref, wc_ref, m1_ref, ar_ref, ai_ref, *, N1, ct):
    wc, m1 = wc_ref[...], m1_ref[...]
    for g in range(ct // LANES):
        sl = slice(g * LANES, (g + 1) * LANES)
        z = _mm(u_ref[:, sl], wc)
        zs = jnp.concatenate([z[:, :LANES], z[:, LANES:]], axis=0).astype(BF16)
        a = _mm(m1, zs)
        ar_ref[:, sl] = a[:N1].astype(BF16)
        ai_ref[:, sl] = a[N1:].astype(BF16)


def _fft2_kernel(ar_ref, ai_ref, m2_ref, o_ref):
    for p in range(FFT2_P):
        x = jnp.concatenate([ar_ref[p], ai_ref[p]], axis=0)
        o_ref[:, p * 512:(p + 1) * 512] = _mm(m2_ref[p], x).astype(BF16)


def _fourier(u, consts, B, S):
    wc, m1, m2, N1 = consts
    N2 = FFT_N2
    ct = FFT1_COLS
    P = FFT2_P
    W = N2 * 512
    u2 = u.reshape(B * N1, W)
    sds = jax.ShapeDtypeStruct((B * N1, W), BF16)
    ar, ai = pl.pallas_call(
        functools.partial(_fft1_kernel, N1=N1, ct=ct),
        grid=(B, W // ct),
        in_specs=[
            pl.BlockSpec((N1, ct), lambda b, j: (b, j)),
            pl.BlockSpec(wc.shape, lambda b, j: (0, 0)),
            pl.BlockSpec(m1.shape, lambda b, j: (0, 0)),
        ],
        out_specs=[pl.BlockSpec((N1, ct), lambda b, j: (b, j))] * 2,
        out_shape=[sds, sds],
        compiler_params=_cparams(("parallel", "parallel")),
        name="fft_stage1",
    )(u2, wc, m1)
    ar3 = ar.reshape(B * N1, N2, 512)
    ai3 = ai.reshape(B * N1, N2, 512)
    y = pl.pallas_call(
        _fft2_kernel,
        grid=(B, N1 // P),
        in_specs=[
            pl.BlockSpec((P, N2, 512), lambda b, p: (b * (N1 // P) + p, 0, 0)),
            pl.BlockSpec((P, N2, 512), lambda b, p: (b * (N1 // P) + p, 0, 0)),
            pl.BlockSpec((P, N2, 2 * N2), lambda b, p: (p, 0, 0)),
        ],
        out_specs=pl.BlockSpec((N2, P * 512), lambda b, p: (b, p)),
        out_shape=jax.ShapeDtypeStruct((B * N2, N1 * 512), BF16),
        compiler_params=_cparams(("parallel", "parallel")),
        name="fft_stage2",
    )(ar3, ai3, m2)
    return y.reshape(B * S, 512)


def _merge_kernel(x_ref, oa_ref, ob_ref, oc_ref, od_ref, g1_ref, wg_ref, wb_ref, wo_ref, g2_ref, wr_ref,
                  x1_ref, h2_ref, aff_ref):
    x = x_ref[...]
    h = _rms(x, g1_ref[...]).astype(BF16)
    branches = (oa_ref, ob_ref, oc_ref, od_ref)
    mixed = None
    for n in range(N_BRANCHES):
        term = jax.nn.sigmoid(_mm(h, wg_ref[n])) * _mm(branches[n][...], wb_ref[n])
        mixed = term if mixed is None else mixed + term
    x1 = x + _mm(mixed.astype(BF16), wo_ref[...])
    x1_ref[...] = x1
    h2 = _rms(x1, g2_ref[...]).astype(BF16)
    h2_ref[...] = h2
    logits = _nt(wr_ref[...], h2)
    mx = jnp.max(logits, axis=0, keepdims=True)
    ex = jnp.exp(logits - mx)
    aff_ref[...] = ex / jnp.sum(ex, axis=0, keepdims=True)


def _merge(x2, oa, ob, oc, od, g1, wg, wb, wo, g2, wr_t):
    T = x2.shape[0]
    tm = TOKEN_TILE
    tok = lambda w: pl.BlockSpec((tm, w), lambda i: (i, 0))
    c2 = lambda i: (0, 0)
    c3 = lambda i: (0, 0, 0)
    return pl.pallas_call(
        _merge_kernel,
        grid=(T // tm,),
        in_specs=[
            tok(D_MODEL), tok(512), tok(512), tok(512), tok(512),
            pl.BlockSpec((1, D_MODEL), c2),
            pl.BlockSpec((N_BRANCHES, D_MODEL, D_MODEL), c3),
            pl.BlockSpec((N_BRANCHES, BRANCH_WIDTH, D_MODEL), c3),
            pl.BlockSpec((D_MODEL, D_MODEL), c2),
            pl.BlockSpec((1, D_MODEL), c2),
            pl.BlockSpec((N_EXPERTS, D_MODEL), c2),
        ],
        out_specs=[tok(D_MODEL), tok(D_MODEL), pl.BlockSpec((N_EXPERTS, tm), lambda i: (0, i))],
        out_shape=[jax.ShapeDtypeStruct((T, D_MODEL), F32), jax.ShapeDtypeStruct((T, D_MODEL), BF16),
                   jax.ShapeDtypeStruct((N_EXPERTS, T), F32)],
        compiler_params=_cparams(("parallel",)),
        name="merge",
    )(x2, oa, ob, oc, od, g1, wg, wb, wo, g2, wr_t)


def _cap_rows(T):
    cap = CAPACITY_FACTOR * T // N_EXPERTS
    ntiles = T // MOE_TILE
    need = cap + BF16_SUBLANES * ntiles + MOE_SUB
    return cap, ntiles, -(-need // FFN_ROWS) * FFN_ROWS


def _route_kernel(aff_ref, tri_ref, rank_ref, cnt_ref, off_ref, *, T, cap, ntiles):
    tk = MOE_TILE
    bits = pltpu.bitcast(aff_ref[...], jnp.int32)
    capf = jnp.float32(cap)

    def count(pred):
        return jnp.sum(jnp.where(pred, 1.0, 0.0), axis=-1, keepdims=True)

    def vbody(i, v):
        cand = v | lax.shift_left(jnp.int32(1), 30 - i)
        return jnp.where(count(bits >= cand) >= capf, cand, v)

    v = lax.fori_loop(0, 31, vbody, jnp.zeros((N_EXPERTS, 1), jnp.int32))
    need = capf - count(bits > v)
    ties = bits == v
    idx = lax.broadcasted_iota(jnp.int32, (N_EXPERTS, T), 1)
    nbits = max(1, (T - 1).bit_length())

    def jbody(i, j0):
        cand = j0 | lax.shift_left(jnp.int32(1), nbits - 1 - i)
        return jnp.where(count(ties & (idx < cand)) < need, cand, j0)

    j0 = lax.fori_loop(0, nbits, jbody, jnp.zeros((N_EXPERTS, 1), jnp.int32))
    sel = (bits > v) | (ties & (idx <= j0))
    tri = tri_ref[...]
    cnt_ref[...] = jnp.zeros_like(cnt_ref)
    for i in range(ntiles):
        sl = slice(i * tk, (i + 1) * tk)
        s_i = sel[:, sl]
        sf = jnp.where(s_i, 1.0, 0.0)
        r = _mm(sf.astype(BF16), tri)
        rank_ref[:, sl] = jnp.where(s_i, r, -1.0)
        cnt_ref[:, i:i + 1] = jnp.sum(sf, axis=-1, keepdims=True)
    cnt = cnt_ref[...]
    units = jnp.ceil(cnt * (1.0 / BF16_SUBLANES))
    off_ref[...] = (_mm(units.astype(BF16), tri[:LANES, :LANES]) * BF16_SUBLANES).astype(jnp.int32)


def _route(aff_t, tri, T):
    cap, ntiles, _ = _cap_rows(T)
    assert ntiles <= LANES and MOE_TILE // BF16_SUBLANES <= 256
    full = lambda s: pl.BlockSpec(s, lambda: tuple(0 for _ in s))
    rank, cnt, off = pl.pallas_call(
        functools.partial(_route_kernel, T=T, cap=cap, ntiles=ntiles),
        in_specs=[full((N_EXPERTS, T)), full(tri.shape)],
        out_specs=[full((N_EXPERTS, T)), full((N_EXPERTS, LANES)), full((N_EXPERTS, LANES))],
        out_shape=[jax.ShapeDtypeStruct((N_EXPERTS, T), F32), jax.ShapeDtypeStruct((N_EXPERTS, LANES), F32),
                   jax.ShapeDtypeStruct((N_EXPERTS, LANES), jnp.int32)],
        compiler_params=_cparams(()),
        name="moe_route",
    )(aff_t, tri)
    return rank, cnt.astype(jnp.int32), off


def _gather_kernel(cnt_ref, off_ref, rank_ref, aff_ref, h_ref, xe_ref, gate_ref):
    e, i = pl.program_id(0), pl.program_id(1)
    R = MOE_SUB

    @pl.when(i == 0)
    def _():
        xe_ref[...] = jnp.zeros_like(xe_ref)
        gate_ref[...] = jnp.zeros_like(gate_ref)

    c = cnt_ref[e, i]
    off = off_ref[e, i]
    rank = rank_ref[0]
    aff = aff_ref[0]
    jrow = lax.broadcasted_iota(jnp.int32, (R, MOE_TILE), 0).astype(F32)

    def body(r, carry):
        hit = rank == jrow + (r * R).astype(F32)
        x = _mm(jnp.where(hit, 1.0, 0.0).astype(BF16), h_ref[...])
        g = jnp.sum(jnp.where(hit, aff, 0.0), axis=-1, keepdims=True)
        r0 = pl.multiple_of(off + r * R, BF16_SUBLANES)
        xe_ref[0, pl.ds(r0, R), :] = x.astype(BF16)
        gate_ref[0, pl.ds(r0, R), :] = jnp.broadcast_to(g, (R, LANES))
        return carry

    lax.fori_loop(0, (c + R - 1) // R, body, 0)


def _gather(cnt, off, rank3, aff3, h2, T):
    _, ntiles, rows = _cap_rows(T)
    tk = MOE_TILE
    return pl.pallas_call(
        _gather_kernel,
        grid_spec=pltpu.PrefetchScalarGridSpec(
            num_scalar_prefetch=2,
            grid=(N_EXPERTS, ntiles),
            in_specs=[
                pl.BlockSpec((1, 1, tk), lambda e, i, c, o: (e, 0, i)),
                pl.BlockSpec((1, 1, tk), lambda e, i, c, o: (e, 0, i)),
                pl.BlockSpec((tk, D_MODEL), lambda e, i, c, o: (i, 0)),
            ],
            out_specs=[
                pl.BlockSpec((1, rows, D_MODEL), lambda e, i, c, o: (e, 0, 0)),
                pl.BlockSpec((1, rows, LANES), lambda e, i, c, o: (e, 0, 0)),
            ],
        ),
        out_shape=[jax.ShapeDtypeStruct((N_EXPERTS, rows, D_MODEL), BF16),
                   jax.ShapeDtypeStruct((N_EXPERTS, rows, LANES), F32)],
        compiler_params=_cparams(("parallel", "arbitrary")),
        name="moe_gather",
    )(cnt, off, rank3, aff3, h2)


def _ffn_kernel(used_ref, x_ref, gate_ref, wg_ref, wu_ref, wd_ref, y_ref):
    live = pl.program_id(1) * FFN_ROWS < used_ref[pl.program_id(0)]

    @pl.when(live)
    def _():
        x = x_ref[0]
        a = _mm(x, wg_ref[0])
        u = _mm(x, wu_ref[0])
        hmid = (a * jax.nn.sigmoid(a) * u).astype(BF16)
        y_ref[0] = (_mm(hmid, wd_ref[0]) * gate_ref[0][:, :1]).astype(BF16)

    @pl.when(jnp.logical_not(live))
    def _():
        y_ref[...] = jnp.zeros_like(y_ref)


def _ffn(used, xe, gate, wg, wu, wd):
    E, rows, _ = xe.shape
    wspec = lambda s: pl.BlockSpec((1,) + s, lambda e, j, u: (e, 0, 0))
    return pl.pallas_call(
        _ffn_kernel,
        grid_spec=pltpu.PrefetchScalarGridSpec(
            num_scalar_prefetch=1,
            grid=(E, rows // FFN_ROWS),
            in_specs=[
                pl.BlockSpec((1, FFN_ROWS, D_MODEL), lambda e, j, u: (e, j, 0)),
                pl.BlockSpec((1, FFN_ROWS, LANES), lambda e, j, u: (e, j, 0)),
                wspec((D_MODEL, EXPERT_FF)), wspec((D_MODEL, EXPERT_FF)), wspec((EXPERT_FF, D_MODEL)),
            ],
            out_specs=pl.BlockSpec((1, FFN_ROWS, D_MODEL), lambda e, j, u: (e, j, 0)),
        ),
        out_shape=jax.ShapeDtypeStruct(xe.shape, BF16),
        compiler_params=_cparams(("parallel", "parallel")),
        name="moe_ffn",
    )(used, xe, gate, wg, wu, wd)


def _scatter_kernel(cnt_ref, off_ref, x_ref, rank_ref, ye_ref, o_ref, seg_ref, oh_ref, extra_ref, sem, xsem):
    i = pl.program_id(0)
    R = MOE_SUB
    tk = MOE_TILE

    def seg_copy(e):
        r0 = pl.multiple_of(off_ref[e, i], BF16_SUBLANES)
        return pltpu.make_async_copy(ye_ref.at[e, pl.ds(r0, R)], seg_ref.at[pl.ds(e * R, R)], sem.at[e])

    for e in range(N_EXPERTS):
        seg_copy(e).start()
    jlane = lax.broadcasted_iota(jnp.int32, (tk, R), 1).astype(F32)
    for e in range(N_EXPERTS):
        oh_ref[:, e * R:(e + 1) * R] = jnp.where(rank_ref[:, e:e + 1] == jlane, 1.0, 0.0).astype(BF16)
    for e in range(N_EXPERTS):
        seg_copy(e).wait()
    o_ref[...] = x_ref[...] + _mm(oh_ref[...], seg_ref[...])

    for e in range(N_EXPERTS):
        c = cnt_ref[e, i]

        def body(r, carry, e=e):
            r0 = pl.multiple_of(off_ref[e, i] + r * R, BF16_SUBLANES)
            cp = pltpu.make_async_copy(ye_ref.at[e, pl.ds(r0, R)], extra_ref, xsem.at[0])
            cp.start()
            hit = rank_ref[:, e:e + 1] == jlane + (r * R).astype(F32)
            cp.wait()
            o_ref[...] += _mm(jnp.where(hit, 1.0, 0.0).astype(BF16), extra_ref[...])
            return carry

        lax.fori_loop(1, (c + R - 1) // R, body, 0)


def _scatter(cnt, off, x1, rank_t, ye, T):
    _, ntiles, rows = _cap_rows(T)
    tk = MOE_TILE
    R = MOE_SUB
    return pl.pallas_call(
        _scatter_kernel,
        grid_spec=pltpu.PrefetchScalarGridSpec(
            num_scalar_prefetch=2,
            grid=(ntiles,),
            in_specs=[
                pl.BlockSpec((tk, D_MODEL), lambda i, c, o: (i, 0)),
                pl.BlockSpec((tk, N_EXPERTS), lambda i, c, o: (i, 0)),
                pl.BlockSpec(memory_space=pl.ANY),
            ],
            out_specs=pl.BlockSpec((tk, D_MODEL), lambda i, c, o: (i, 0)),
            scratch_shapes=[
                pltpu.VMEM((N_EXPERTS * R, D_MODEL), BF16),
                pltpu.VMEM((tk, N_EXPERTS * R), BF16),
                pltpu.VMEM((R, D_MODEL), BF16),
                pltpu.SemaphoreType.DMA((N_EXPERTS,)),
                pltpu.SemaphoreType.DMA((1,)),
            ],
        ),
        out_shape=jax.ShapeDtypeStruct((T, D_MODEL), F32),
        compiler_params=_cparams(("arbitrary",)),
        name="moe_scatter",
    )(cnt, off, x1, rank_t, ye)


def _expert_choice(x1, h2, aff_t, tri, wg, wu, wd):
    T = x1.shape[0]
    rank, cnt, off = _route(aff_t, tri, T)
    xe, gate = _gather(cnt, off, rank.reshape(N_EXPERTS, 1, T), aff_t.reshape(N_EXPERTS, 1, T), h2, T)
    ntiles = T // MOE_TILE
    last_pad = -(-cnt[:, ntiles - 1] // BF16_SUBLANES) * BF16_SUBLANES
    ye = _ffn(off[:, ntiles - 1] + last_pad, xe, gate, wg, wu, wd)
    return _scatter(cnt, off, x1, rank.T, ye, T)


def _pad_heads(w, n_heads, width, pad_to):
    k = w.shape[0]
    w = w.reshape(k, n_heads, width)
    return jnp.pad(w, ((0, 0), (0, 0), (0, pad_to - width))).reshape(k, n_heads * pad_to)


def _pack_layer(l, p):
    w_in = p['w_in'][l]
    edges = np.cumsum([0, 512, 128, 128, B_Q_LORA, B_KV_LORA + B_ROPE, 512, 512, 512, 512, 512, 512])
    a_q, a_k, a_v, b_cq, b_ckv, c_q, c_ff, c_fb, c_i, c_g, d_u = [w_in[:, edges[j]:edges[j + 1]] for j in range(11)]
    zeros = lambda n: jnp.zeros((D_MODEL, n), w_in.dtype)
    b_kpe = jnp.concatenate([zeros(B_NOPE), b_ckv[:, B_KV_LORA:], zeros(PAD_HEAD - B_QK)], axis=1)
    w_main = jnp.concatenate([a_q, c_q, c_ff, c_fb, c_i, c_g, a_k, a_v, b_cq, b_ckv[:, :B_KV_LORA], b_kpe,
                              _swap_rope_halves(b_kpe)], axis=1).astype(BF16)
    kv_up = p['b_kv_up'][l].reshape(B_KV_LORA, B_HEADS, B_NOPE + B_V)
    pad_g = lambda g: jnp.pad(g, (0, PAD_HEAD - B_QK)).reshape(1, PAD_HEAD)
    b_wq = _pad_heads(p['b_q_up'][l], B_HEADS, B_QK, PAD_HEAD)
    b_wqsw = _swap_rope_halves(b_wq.reshape(B_Q_LORA, B_HEADS, PAD_HEAD)).reshape(B_Q_LORA, B_HEADS * PAD_HEAD)
    b_gq, b_gk = pad_g(p['b_q_norm_g'][l]), pad_g(p['b_k_norm_g'][l])
    sink = p['a_sink'][l].astype(F32).reshape(A_KV_HEADS, A_HEADS // A_KV_HEADS)
    b_bound = math.sqrt(B_QK) * jnp.max(jnp.abs(p['b_q_norm_g'][l])) * jnp.max(jnp.abs(p['b_k_norm_g'][l]))
    lane = jnp.arange(PAD_HEAD)
    b_qadd = jnp.where(lane == SCORE_LANE, 1.0, 0.0).astype(F32).reshape(1, PAD_HEAD)
    b_kadd = jnp.where(lane == SCORE_LANE, -LOG2E * b_bound, 0.0).astype(F32).reshape(1, PAD_HEAD)
    b_vadd = jnp.tile(jnp.where(lane == SUM_LANE, 1.0, 0.0).astype(F32), B_HEADS).reshape(1, B_HEADS * PAD_HEAD)
    a_bound, a_bias, a_sink_term = _win_fast_tables(p['a_q_norm_g'][l], p['a_k_norm_g'][l], p['a_sink'][l])
    return dict(
        a_bound=a_bound, a_bias=a_bias, a_sink_term=a_sink_term,
        a_gq_t=(jnp.tile(p['a_q_norm_g'][l], A_HEADS) * (LOG2E * A_HEAD_DIM ** -0.5)).reshape(1, A_HEADS * A_HEAD_DIM),
        a_gk_t=jnp.tile(p['a_k_norm_g'][l], A_KV_HEADS).reshape(1, A_KV_HEADS * A_HEAD_DIM),
        b_bound=b_bound, b_adds=(b_qadd, b_kadd, b_vadd),
        g_mix=p['norm_mix_g'][l].reshape(1, D_MODEL), w_main=w_main, w_u=d_u.astype(BF16),
        a_gq=p['a_q_norm_g'][l].reshape(1, A_HEAD_DIM), a_gk=p['a_k_norm_g'][l].reshape(1, A_HEAD_DIM),
        a_sink=jnp.repeat(sink, WINDOW, axis=1)[:, :, None],
        b_gqa=p['b_q_a_norm_g'][l].reshape(1, B_Q_LORA), b_gkva=p['b_kv_a_norm_g'][l].reshape(1, B_KV_LORA),
        b_wq=b_wq.astype(BF16), b_wqsw=b_wqsw.astype(BF16),
        b_wk=_pad_heads(kv_up[:, :, :B_NOPE].reshape(B_KV_LORA, -1), B_HEADS, B_NOPE, PAD_HEAD).astype(BF16),
        b_wv=_pad_heads(kv_up[:, :, B_NOPE:].reshape(B_KV_LORA, -1), B_HEADS, B_V, PAD_HEAD).astype(BF16),
        b_gq=b_gq, b_gk=b_gk, b_gqsw=_swap_rope_halves(b_gq), b_gksw=_swap_rope_halves(b_gk),
        c_gn=p['c_out_norm_g'][l].reshape(1, C_DV),
        w_gate=p['w_gate'][l].astype(BF16), w_branch=p['w_branch'][l].astype(BF16), w_out=p['w_out'][l].astype(BF16),
        g_ffn=p['norm_ffn_g'][l].reshape(1, D_MODEL), w_router_t=p['w_router'][l].T.astype(BF16),
        w_e_gate=p['w_e_gate'][l].astype(BF16), w_e_up=p['w_e_up'][l].astype(BF16), w_e_down=p['w_e_down'][l].astype(BF16),
    )


def _layer(x2, B, S, lp, lb, cst):
    proj, u = _in_proj(x2, lp['g_mix'], lp['w_main'], lp['w_u'])
    o_a = lax.cond(lp['a_bound'] <= FAST_SOFTMAX_MAX_BOUND,
                   lambda pr: _win_attn_fast(pr, lp['a_gq_t'], lp['a_gk_t'], cst['win'], lp['a_bias'], lp['a_sink_term'], B, S),
                   lambda pr: _win_attn(pr, lp['a_gq'], lp['a_gk'], lp['a_sink'], B, S), proj)
    q, k, v = _mla_prep(proj, lp, cst['rope'][S], S)
    o_b = lax.cond(lp['b_bound'] <= FAST_SOFTMAX_MAX_BOUND,
                   lambda q, k, v: _flash(q, k, v, B, S, False),
                   lambda q, k, v: _flash(q, k, v, B, S, True), q, k, v)
    o_f = _hgrn(proj, lb[0:1], cst['hgrn_f'], B, S, False)
    o_c = _hgrn(proj, lb[1:2], cst['hgrn_b'], B, S, True, o_f, lp['c_gn'])
    o_d = _fourier(u, cst['fft'][S], B, S)
    x1, h2, aff_t = _merge(x2, o_a, o_b, o_c, o_d, lp['g_mix'], lp['w_gate'], lp['w_branch'], lp['w_out'],
                           lp['g_ffn'], lp['w_router_t'])
    return _expert_choice(x1, h2, aff_t, cst['tri'], lp['w_e_gate'], lp['w_e_up'], lp['w_e_down'])


def kernel(x_prompt, x_sample, norm_mix_g, w_in, a_q_norm_g, a_k_norm_g, a_sink, b_q_a_norm_g, b_q_up,
           b_kv_a_norm_g, b_kv_up, b_q_norm_g, b_k_norm_g, c_lb_logits, c_out_norm_g, w_branch, w_gate, w_out,
           norm_ffn_g, w_router, w_e_gate, w_e_up, w_e_down):
    p = dict(norm_mix_g=norm_mix_g, w_in=w_in, a_q_norm_g=a_q_norm_g, a_k_norm_g=a_k_norm_g, a_sink=a_sink,
             b_q_a_norm_g=b_q_a_norm_g, b_q_up=b_q_up, b_kv_a_norm_g=b_kv_a_norm_g, b_kv_up=b_kv_up,
             b_q_norm_g=b_q_norm_g, b_k_norm_g=b_k_norm_g, c_out_norm_g=c_out_norm_g, w_branch=w_branch,
             w_gate=w_gate, w_out=w_out, norm_ffn_g=norm_ffn_g, w_router=w_router, w_e_gate=w_e_gate,
             w_e_up=w_e_up, w_e_down=w_e_down)
    lb_w = jax.nn.softmax(c_lb_logits.astype(F32), axis=0)
    lb_all = jnp.cumsum(lb_w, axis=0) - lb_w[0:1]
    groups = [x_prompt, x_sample]
    seqs = sorted({g.shape[1] for g in groups})
    tk = MOE_TILE
    cst = dict(
        rope={S: _rope_tables(S) for S in seqs},
        fft={S: _fft_consts(S) for S in seqs},
        hgrn_f=_hgrn_consts(HGRN_CHUNK, False), hgrn_b=_hgrn_consts(HGRN_CHUNK, True),
        win=_win_fast_consts(),
        tri=jnp.asarray(np.triu(np.ones((tk, tk), np.float32), 1), BF16),
    )
    layers = [_pack_layer(l, p) for l in range(DEPTH)]
    outs = []
    for xg in groups:
        B, S, _ = xg.shape
        y = xg.reshape(B * S, D_MODEL)
        for l in range(DEPTH):
            y = _layer(y, B, S, layers[l], lb_all[l], cst)
        outs.append(y.reshape(B, S, D_MODEL))
    return tuple(outs)
```

```python
import functools
import math

import numpy as np
import jax
import jax.numpy as jnp
from jax import lax
from jax.experimental import pallas as pl
from jax.experimental.pallas import tpu as pltpu

F32 = jnp.float32
BF16 = jnp.bfloat16

D_MODEL = 1024
DEPTH = 2
A_HEADS, A_KV_HEADS, A_HEAD_DIM, WINDOW = 8, 2, 64, 128
B_HEADS, B_Q_LORA, B_KV_LORA, B_NOPE, B_ROPE, B_V = 8, 256, 128, 64, 32, 64
B_QK = B_NOPE + B_ROPE
ROPE_THETA = 10000.0
C_HEADS, C_DK, C_DV = 4, 128, 128
D_GROUPS, D_GROUP_DIM = 4, 128
N_BRANCHES, BRANCH_WIDTH = 4, 512
N_EXPERTS, EXPERT_FF, CAPACITY_FACTOR = 16, 1024, 2
EPS = 1e-6
MASK_VALUE = -1e30
TINY = 1e-30
LOG2E = 1.4426950408889634

SCORE_LANE = B_QK
SUM_LANE = B_V
FAST_SOFTMAX_MAX_BOUND = 30.0

LANES = 128
BF16_SUBLANES = 16
V7X_VMEM_LIMIT = 56 * 1024 * 1024

TOKEN_TILE = 512
PAD_HEAD = 128
HGRN_CHUNK = 128
FFT_N2 = 128
FFT1_COLS = 4096
FFT1_ROWS = 64
FFT2_P = 4
FLASH_TQ, FLASH_TK, FLASH_UNROLL = 256, 512, 5
MOE_TILE = 512
MOE_SUB = 128
FFN_ROWS = 512

OFF_AQ, OFF_CQ, OFF_CFF, OFF_CFB, OFF_CI, OFF_CG = 0, 512, 1024, 1536, 2048, 2560
OFF_AK, OFF_AV, OFF_BCQ, OFF_BLAT, OFF_BKPE, OFF_BKPESW = 3072, 3200, 3328, 3584, 3712, 3840
PROJ_W = 3968


def _cparams(sem, vmem=V7X_VMEM_LIMIT):
    return pltpu.CompilerParams(dimension_semantics=sem, vmem_limit_bytes=vmem)


def _nt(a, b):
    return lax.dot_general(a, b, (((1,), (1,)), ((), ())), preferred_element_type=F32)


def _mm(a, b):
    return jnp.dot(a, b, preferred_element_type=F32)


def _rms(x, g):
    return x * lax.rsqrt(jnp.mean(x * x, axis=-1, keepdims=True) + EPS) * g


def _in_proj_kernel(x_ref, g_ref, w_ref, wu_ref, p_ref, u_ref):
    h = _rms(x_ref[...], g_ref[...]).astype(BF16)
    for c0 in range(0, PROJ_W, 512):
        c1 = min(c0 + 512, PROJ_W)
        p_ref[:, c0:c1] = _mm(h, w_ref[:, c0:c1]).astype(BF16)
    u_ref[...] = _mm(h, wu_ref[...]).astype(BF16)


def _in_proj(x2, g, w_main, w_u):
    T = x2.shape[0]
    tm = TOKEN_TILE
    return pl.pallas_call(
        _in_proj_kernel,
        grid=(T // tm,),
        in_specs=[
            pl.BlockSpec((tm, D_MODEL), lambda i: (i, 0)),
            pl.BlockSpec((1, D_MODEL), lambda i: (0, 0)),
            pl.BlockSpec((D_MODEL, PROJ_W), lambda i: (0, 0)),
            pl.BlockSpec((D_MODEL, 512), lambda i: (0, 0)),
        ],
        out_specs=[
            pl.BlockSpec((tm, PROJ_W), lambda i: (i, 0)),
            pl.BlockSpec((tm, 512), lambda i: (i, 0)),
        ],
        out_shape=[jax.ShapeDtypeStruct((T, PROJ_W), BF16), jax.ShapeDtypeStruct((T, 512), BF16)],
        compiler_params=_cparams(("parallel",)),
        name="in_proj",
    )(x2, g, w_main, w_u)


def _win_attn_kernel(q_ref, kp_ref, kc_ref, kn_ref, vp_ref, vc_ref, vn_ref, gq_ref, gk_ref, sink_ref, o_ref, *, nb):
    n = pl.program_id(1)
    W = WINDOW
    G = A_HEADS // A_KV_HEADS
    q = q_ref[...].astype(F32)
    kcat = jnp.concatenate([kp_ref[...], kc_ref[...], kn_ref[...]], axis=0).astype(F32)
    vcat = jnp.concatenate([vp_ref[...], vc_ref[...], vn_ref[...]], axis=0)
    gq = gq_ref[...]
    gk = gk_ref[...]
    row = lax.broadcasted_iota(jnp.int32, (G * W, 3 * W), 0)
    col = lax.broadcasted_iota(jnp.int32, (G * W, 3 * W), 1)
    qi = row % W
    dist = jnp.abs(col - W - qi)
    valid = dist <= W
    valid = valid & ((col >= W) | (n > 0)) & ((col < 2 * W) | (n < nb - 1))
    distf = dist.astype(F32)
    hrow = lax.broadcasted_iota(jnp.int32, (G * W, 1), 0) // W
    for g in range(A_KV_HEADS):
        qs = []
        for j in range(G):
            h = g * G + j
            qs.append(_rms(q[:, h * A_HEAD_DIM:(h + 1) * A_HEAD_DIM], gq))
        qg = (jnp.concatenate(qs, axis=0) * (A_HEAD_DIM ** -0.5)).astype(BF16)
        kg = _rms(kcat[:, g * A_HEAD_DIM:(g + 1) * A_HEAD_DIM], gk).astype(BF16)
        vg = vcat[:, g * A_HEAD_DIM:(g + 1) * A_HEAD_DIM]
        slope = jnp.exp2(-(hrow + (g * G + 1)).astype(F32))
        s = _nt(qg, kg) - slope * distf
        s = jnp.where(valid, s, MASK_VALUE)
        sk = sink_ref[g]
        m = jnp.maximum(jnp.max(s, axis=-1, keepdims=True), sk)
        p = jnp.exp(s - m)
        den = jnp.sum(p, axis=-1, keepdims=True) + jnp.exp(sk - m)
        o = _mm(p.astype(BF16), vg) / den
        for j in range(G):
            h = g * G + j
            o_ref[:, h * A_HEAD_DIM:(h + 1) * A_HEAD_DIM] = o[j * W:(j + 1) * W].astype(BF16)


def _win_fast_consts():
    d, G = A_HEAD_DIM, A_HEADS // A_KV_HEADS
    qw = A_HEADS * d
    lane = np.arange(qw)
    bd = (lane[:, None] // d == lane[None, :] // d).astype(np.float32) / d
    l128 = np.arange(LANES)
    dup = np.stack([(l128[:, None] == g * d + l128[None, :] % d) for g in range(A_KV_HEADS)]).astype(np.float32)
    sel = np.stack([np.stack([(l128[:, None] == g * d + l128[None, :] - par * d) & (l128[None, :] // d == par)
                              for par in range(2)]) for g in range(A_KV_HEADS)]).astype(np.float32)
    return jnp.asarray(bd, BF16), jnp.asarray(dup, BF16), jnp.asarray(sel, BF16)


def _win_fast_tables(gq, gk, sink):
    W = WINDOW
    bound = math.sqrt(A_HEAD_DIM) * jnp.max(jnp.abs(gq)) * jnp.max(jnp.abs(gk))
    sink = sink.astype(F32)
    m = jnp.maximum(bound, sink)
    slopes = 2.0 ** (-8.0 * jnp.arange(1, A_HEADS + 1, dtype=F32) / A_HEADS)
    r = jnp.arange(W)[:, None]
    c = jnp.arange(3 * W)[None, :]
    dist = jnp.abs(c - W - r)
    inside = dist <= W
    variants = []
    for first, last in ((False, False), (True, False), (False, True), (True, True)):
        valid = inside & ((c >= W) | (not first)) & ((c < 2 * W) | (not last))
        b = LOG2E * (-slopes[:, None, None] * dist[None].astype(F32) - m[:, None, None])
        variants.append(jnp.where(valid[None], b, MASK_VALUE))
    sink_term = jnp.exp2(LOG2E * (sink - m))
    return bound, jnp.stack(variants), jnp.broadcast_to(sink_term[:, None, None], (A_HEADS, 1, LANES))


def _win_fast_kernel(q_ref, kp_ref, kc_ref, kn_ref, vp_ref, vc_ref, vn_ref, gq_ref, gk_ref, bd_ref, dup_ref, sel_ref,
                     bias_ref, st_ref, o_ref):
    d = A_HEAD_DIM
    q = q_ref[...].astype(F32)
    bd = bd_ref[...]
    qf = q * lax.rsqrt(_mm((q * q).astype(BF16), bd) + EPS) * gq_ref[...]
    kcat = jnp.concatenate([kp_ref[...], kc_ref[...], kn_ref[...]], axis=0).astype(F32)
    kss = _mm((kcat * kcat).astype(BF16), bd[:LANES, :LANES])
    kn = (kcat * lax.rsqrt(kss + EPS) * gk_ref[...]).astype(BF16)
    vcat = jnp.concatenate([vp_ref[...], vc_ref[...], vn_ref[...]], axis=0)
    ones = jnp.ones((3 * WINDOW, LANES), BF16)
    lane = lax.broadcasted_iota(jnp.int32, (1, LANES), 1)
    half = [jnp.where(lane < d, 1.0, 0.0), jnp.where(lane >= d, 1.0, 0.0)]
    G = A_HEADS // A_KV_HEADS
    for g in range(A_KV_HEADS):
        kk = _mm(kn, dup_ref[g]).astype(BF16)
        rhs = [jnp.concatenate([_mm(vcat, sel_ref[g, par]).astype(BF16), ones], axis=1) for par in range(2)]
        for jj in range(G // 2):
            pair = g * (G // 2) + jj
            qp = qf[:, pair * LANES:(pair + 1) * LANES]
            acc = None
            for par in range(2):
                h = 2 * pair + par
                s = _nt((qp * half[par]).astype(BF16), kk) + bias_ref[0, h]
                res = _mm(jnp.exp2(s).astype(BF16), rhs[par])
                o = res[:, :LANES] / (res[:, LANES:] + st_ref[h])
                acc = o if acc is None else acc + o
            o_ref[:, pair * LANES:(pair + 1) * LANES] = acc.astype(BF16)


def _win_attn_fast(proj, gq_t, gk_t, consts, bias, sink_term, B, S):
    T = B * S
    W = WINDOW
    nb = S // W
    kcol, vcol = OFF_AK // 128, OFF_AV // 128
    bd, dup, sel = consts
    cur = lambda b, n: b * nb + n
    prev = lambda b, n: b * nb + jnp.maximum(n - 1, 0)
    nxt = lambda b, n: b * nb + jnp.minimum(n + 1, nb - 1)
    spec = lambda rowf, c: pl.BlockSpec((W, 128), lambda b, n: (rowf(b, n), c))
    variant = lambda b, n: ((n == 0).astype(jnp.int32) + 2 * (n == nb - 1).astype(jnp.int32), 0, 0, 0)
    c2 = lambda b, n: (0, 0)
    c3 = lambda b, n: (0, 0, 0)
    return pl.pallas_call(
        _win_fast_kernel,
        grid=(B, nb),
        in_specs=[
            pl.BlockSpec((W, 512), lambda b, n: (cur(b, n), OFF_AQ // 512)),
            spec(prev, kcol), spec(cur, kcol), spec(nxt, kcol),
            spec(prev, vcol), spec(cur, vcol), spec(nxt, vcol),
            pl.BlockSpec((1, 512), c2), pl.BlockSpec((1, LANES), c2),
            pl.BlockSpec(bd.shape, c2), pl.BlockSpec(dup.shape, c3), pl.BlockSpec(sel.shape, lambda b, n: (0, 0, 0, 0)),
            pl.BlockSpec((1, A_HEADS, W, 3 * W), variant),
            pl.BlockSpec((A_HEADS, 1, LANES), c3),
        ],
        out_specs=pl.BlockSpec((W, 512), lambda b, n: (cur(b, n), 0)),
        out_shape=jax.ShapeDtypeStruct((T, 512), BF16),
        compiler_params=_cparams(("parallel", "parallel")),
        name="win_attn_fast",
    )(proj, proj, proj, proj, proj, proj, proj, gq_t, gk_t, bd, dup, sel, bias, sink_term)


def _win_attn(proj, gq, gk, sink_cols, B, S):
    T = B * S
    W = WINDOW
    nb = S // W
    kcol, vcol = OFF_AK // 128, OFF_AV // 128

    def cur(b, n):
        return b * nb + n

    def prev(b, n):
        return b * nb + jnp.maximum(n - 1, 0)

    def nxt(b, n):
        return b * nb + jnp.minimum(n + 1, nb - 1)

    def spec(rowf, c):
        return pl.BlockSpec((W, 128), lambda b, n: (rowf(b, n), c))

    return pl.pallas_call(
        functools.partial(_win_attn_kernel, nb=nb),
        grid=(B, nb),
        in_specs=[
            pl.BlockSpec((W, 512), lambda b, n: (cur(b, n), OFF_AQ // 512)),
            spec(prev, kcol), spec(cur, kcol), spec(nxt, kcol),
            spec(prev, vcol), spec(cur, vcol), spec(nxt, vcol),
            pl.BlockSpec((1, A_HEAD_DIM), lambda b, n: (0, 0)),
            pl.BlockSpec((1, A_HEAD_DIM), lambda b, n: (0, 0)),
            pl.BlockSpec((A_KV_HEADS, 4 * W, 1), lambda b, n: (0, 0, 0)),
        ],
        out_specs=pl.BlockSpec((W, 512), lambda b, n: (cur(b, n), 0)),
        out_shape=jax.ShapeDtypeStruct((T, 512), BF16),
        compiler_params=_cparams(("parallel", "parallel")),
        name="win_attn",
    )(proj, proj, proj, proj, proj, proj, proj, gq, gk, sink_cols)


def _rope_tables(S):
    half = B_ROPE // 2
    inv = 1.0 / (ROPE_THETA ** (np.arange(half, dtype=np.float32) / half))
    ang = np.arange(S, dtype=np.float32)[:, None] * inv[None, :]
    cos, sin = np.cos(ang), np.sin(ang)
    cosf = np.ones((S, PAD_HEAD), np.float32)
    sinf = np.zeros((S, PAD_HEAD), np.float32)
    cosf[:, B_NOPE:B_NOPE + half] = cos
    cosf[:, B_NOPE + half:B_NOPE + 2 * half] = cos
    sinf[:, B_NOPE:B_NOPE + half] = -sin
    sinf[:, B_NOPE + half:B_NOPE + 2 * half] = sin
    return jnp.asarray(cosf), jnp.asarray(sinf)


def _swap_rope_halves(w):
    half = B_ROPE // 2
    z = jnp.zeros_like(w)
    z = z.at[..., B_NOPE:B_NOPE + half].set(w[..., B_NOPE + half:B_NOPE + 2 * half])
    return z.at[..., B_NOPE + half:B_NOPE + 2 * half].set(w[..., B_NOPE:B_NOPE + half])


def _mla_prep_kernel(cq_ref, lat_ref, kpe_ref, kpesw_ref, gqa_ref, gkva_ref, wq_ref, wqsw_ref, wk_ref, wv_ref,
                     gq_ref, gqsw_ref, gk_ref, gksw_ref, cos_ref, sin_ref, qadd_ref, kadd_ref, vadd_ref,
                     q_ref, k_ref, v_ref):
    cosf, sinf = cos_ref[...], sin_ref[...]
    qscale = LOG2E * B_QK ** -0.5
    g1q, g2q = gq_ref[...] * cosf * qscale, gqsw_ref[...] * sinf * qscale
    g1k, g2k = gk_ref[...] * cosf, gksw_ref[...] * sinf

    def inv_rms(x):
        return lax.rsqrt(jnp.sum(x * x, axis=-1, keepdims=True) * (1.0 / B_QK) + EPS)

    cq = _rms(cq_ref[...].astype(F32), gqa_ref[...]).astype(BF16)
    qall = _mm(cq, wq_ref[...])
    qsw = _mm(cq, wqsw_ref[...])
    c = _rms(lat_ref[...].astype(F32), gkva_ref[...]).astype(BF16)
    kall = _mm(c, wk_ref[...])
    v_ref[...] = (_mm(c, wv_ref[...]) + vadd_ref[...]).astype(BF16)
    kpe = kpe_ref[...].astype(F32)
    krot = kpesw_ref[...].astype(F32) * g2k
    qadd, kadd = qadd_ref[...], kadd_ref[...]
    for h in range(B_HEADS):
        sl = slice(h * PAD_HEAD, (h + 1) * PAD_HEAD)
        x = qall[:, sl]
        q_ref[:, sl] = ((x * g1q + qsw[:, sl] * g2q) * inv_rms(x) + qadd).astype(BF16)
        y = kall[:, sl] + kpe
        k_ref[:, sl] = ((y * g1k + krot) * inv_rms(y) + kadd).astype(BF16)


def _mla_prep(proj, lp, tables, S):
    T = proj.shape[0]
    tm = TOKEN_TILE
    nper = S // tm
    HW = B_HEADS * PAD_HEAD
    cst = lambda i: (0, 0)
    tab = pl.BlockSpec((tm, PAD_HEAD), lambda i: (i % nper, 0))
    row = pl.BlockSpec((1, PAD_HEAD), cst)
    col = lambda off: pl.BlockSpec((tm, PAD_HEAD), lambda i: (i, off // PAD_HEAD))
    out = jax.ShapeDtypeStruct((T, HW), BF16)
    return pl.pallas_call(
        _mla_prep_kernel,
        grid=(T // tm,),
        in_specs=[
            pl.BlockSpec((tm, 256), lambda i: (i, OFF_BCQ // 256)),
            col(OFF_BLAT), col(OFF_BKPE), col(OFF_BKPESW),
            pl.BlockSpec((1, B_Q_LORA), cst), pl.BlockSpec((1, B_KV_LORA), cst),
            pl.BlockSpec((B_Q_LORA, HW), cst), pl.BlockSpec((B_Q_LORA, HW), cst),
            pl.BlockSpec((B_KV_LORA, HW), cst), pl.BlockSpec((B_KV_LORA, HW), cst),
            row, row, row, row, tab, tab, row, row, pl.BlockSpec((1, HW), cst),
        ],
        out_specs=[pl.BlockSpec((tm, HW), lambda i: (i, 0))] * 3,
        out_shape=[out, out, out],
        compiler_params=_cparams(("parallel",)),
        name="mla_prep",
    )(proj, proj, proj, proj, lp['b_gqa'], lp['b_gkva'], lp['b_wq'], lp['b_wqsw'], lp['b_wk'], lp['b_wv'],
      lp['b_gq'], lp['b_gqsw'], lp['b_gk'], lp['b_gksw'], *tables, *lp['b_adds'])


def _flash_kernel(q_ref, k_ref, v_ref, o_ref, *, S, online):
    tq, tk = FLASH_TQ, FLASH_TK
    nk = S // tk
    heads = [slice(hh * PAD_HEAD, (hh + 1) * PAD_HEAD) for hh in range(2)]
    qs = [q_ref[:, sl] for sl in heads]
    a0 = jnp.zeros((tq, PAD_HEAD), F32)

    def probs(j, hh):
        r0 = pl.multiple_of(j * tk, tk)
        return jnp.exp2(_nt(qs[hh], k_ref[pl.ds(r0, tk), heads[hh]])).astype(BF16)

    def pv(j, hh, p):
        r0 = pl.multiple_of(j * tk, tk)
        return _mm(p, v_ref[pl.ds(r0, tk), heads[hh]])

    if online:
        def body(j, carry):
            r0 = pl.multiple_of(j * tk, tk)
            out = []
            for hh, sl in enumerate(heads):
                s = _nt(qs[hh], k_ref[pl.ds(r0, tk), sl])
                m, acc = carry[hh]
                mn = jnp.maximum(m, jnp.max(s, axis=-1, keepdims=True))
                out.append((mn, acc * jnp.exp2(m - mn) + _mm(jnp.exp2(s - mn).astype(BF16), v_ref[pl.ds(r0, tk), sl])))
            return tuple(out)

        init = (jnp.full((tq, 1), MASK_VALUE, F32), a0)
        res = lax.fori_loop(0, nk, body, (init, init))
        accs = [res[0][1], res[1][1]]
    else:
        def body(j, carry):
            out = []
            for hh in range(2):
                acc, p_prev = carry[hh]
                out.append((acc + pv(j - 1, hh, p_prev), probs(j, hh)))
            return tuple(out)

        res = lax.fori_loop(1, nk, body, ((a0, probs(0, 0)), (a0, probs(0, 1))), unroll=FLASH_UNROLL)
        accs = [res[hh][0] + pv(nk - 1, hh, res[hh][1]) for hh in range(2)]
    for hh in range(2):
        acc = accs[hh]
        o_ref[:, hh * B_V:(hh + 1) * B_V] = (acc[:, :B_V] / acc[:, SUM_LANE:SUM_LANE + 1]).astype(BF16)


def _flash(q, k, v, B, S, online):
    T = B * S
    tq = FLASH_TQ
    nq = S // tq
    return pl.pallas_call(
        functools.partial(_flash_kernel, S=S, online=online),
        grid=(B, B_HEADS // 2, nq),
        in_specs=[
            pl.BlockSpec((tq, 2 * PAD_HEAD), lambda b, h, i: (b * nq + i, h)),
            pl.BlockSpec((S, 2 * PAD_HEAD), lambda b, h, i: (b, h)),
            pl.BlockSpec((S, 2 * PAD_HEAD), lambda b, h, i: (b, h)),
        ],
        out_specs=pl.BlockSpec((tq, 2 * B_V), lambda b, h, i: (b * nq + i, h)),
        out_shape=jax.ShapeDtypeStruct((T, B_HEADS * B_V), BF16),
        compiler_params=_cparams(("parallel", "parallel", "parallel")),
        name="mla_flash_online" if online else "mla_flash",
    )(q, k, v)


def _hgrn_consts(L, reverse):
    levels = []
    m = 1
    while m < L:
        levels.append(m)
        m *= 2
    nl = len(levels)
    t = np.arange(L)
    wall = np.zeros((nl + 2, L, L), np.float32)
    right = np.zeros((nl, L, 1), np.float32)
    mask = np.zeros((nl + 1, L, L), np.float32)
    for li, m in enumerate(levels):
        blk = t // (2 * m)
        mid = blk * 2 * m + m
        isr = t >= mid
        for tt in range(L):
            if isr[tt]:
                wall[li, tt, mid[tt]:tt + 1] = 1.0
            else:
                wall[li, tt, tt + 1:mid[tt]] = 1.0
        right[li, :, 0] = isr
        mask[li] = (blk[:, None] == blk[None, :]) & isr[:, None] & (~isr[None, :])
    mask[nl] = np.eye(L)
    wall[nl] = np.tril(np.ones((L, L)))
    wall[nl + 1] = np.triu(np.ones((L, L)), 1)
    if reverse:
        wall = wall[:, ::-1, ::-1]
        right = right[:, ::-1]
        mask = mask[:, ::-1, ::-1]
    right = np.broadcast_to(right, (nl, L, C_DK))
    return (jnp.asarray(np.ascontiguousarray(wall.reshape((nl + 2) * L, L)), BF16),
            jnp.asarray(np.ascontiguousarray(right)), jnp.asarray(np.ascontiguousarray(mask)), nl)


def _hgrn_kernel(*refs, L, nl, reverse):
    if reverse:
        q_ref, z_ref, v_ref, lb_ref, wall_ref, right_ref, mask_ref, of_ref, g_ref, gn_ref, o_ref, st_ref = refs
    else:
        q_ref, z_ref, v_ref, lb_ref, wall_ref, right_ref, mask_ref, o_ref, st_ref = refs

    @pl.when(pl.program_id(1) == 0)
    def _():
        st_ref[...] = jnp.zeros_like(st_ref)

    lb = lb_ref[...]
    z = z_ref[...].astype(F32)
    sig = jax.nn.sigmoid(z)
    lf = jnp.log(jnp.maximum(lb + (1.0 - lb) * sig, TINY))
    key = (1.0 - lb) * (1.0 - sig)
    qx = q_ref[...].astype(F32)
    qh = qx * jax.nn.sigmoid(qx)
    hi = lf.astype(BF16)
    lo = (lf - hi.astype(F32)).astype(BF16)
    wall = wall_ref[...]
    e_all = jnp.exp(_mm(wall, hi) + _mm(wall, lo))
    last = 0 if reverse else L - 1
    for h in range(C_HEADS):
        sl = slice(h * C_DK, (h + 1) * C_DK)
        qh_h, key_h, v_h = qh[:, sl], key[:, sl], v_ref[:, sl]
        scores = _nt(qh_h.astype(BF16), key_h.astype(BF16)) * mask_ref[nl]
        q_minus_k = qh_h - key_h
        for li in range(nl):
            x = ((key_h + right_ref[li] * q_minus_k) * e_all[li * L:(li + 1) * L, sl]).astype(BF16)
            scores = scores + _nt(x, x) * mask_ref[li]
        o = _mm(scores.astype(BF16), v_h)
        ep = e_all[nl * L:(nl + 1) * L, sl]
        er = e_all[(nl + 1) * L:(nl + 2) * L, sl]
        st = st_ref[h]
        o = o + _nt((qh_h * ep).astype(BF16), st.astype(BF16))
        st_ref[h] = st * ep[last:last + 1] + _mm(v_h.T, (key_h * er).astype(BF16))
        if reverse:
            tot = o + of_ref[:, sl]
            o_ref[:, sl] = (_rms(tot, gn_ref[...]) * (lambda u: u * jax.nn.sigmoid(u))(g_ref[:, sl].astype(F32))).astype(BF16)
        else:
            o_ref[:, sl] = o


def _hgrn(proj, lb_dir, consts, B, S, reverse, o_fwd=None, gn=None):
    T = B * S
    L = HGRN_CHUNK
    nc = S // L
    wall, right, mask, nl = consts

    def rows(b, c):
        return b * nc + (nc - 1 - c if reverse else c)

    def pspec(off):
        return pl.BlockSpec((L, 512), lambda b, c: (rows(b, c), off // 512))

    cst2 = lambda b, c: (0, 0)
    cst3 = lambda b, c: (0, 0, 0)
    in_specs = [
        pspec(OFF_CQ), pspec(OFF_CFB if reverse else OFF_CFF), pspec(OFF_CI),
        pl.BlockSpec((1, 512), cst2),
        pl.BlockSpec(wall.shape, cst2), pl.BlockSpec(right.shape, cst3), pl.BlockSpec(mask.shape, cst3),
    ]
    args = [proj, proj, proj, lb_dir, wall, right, mask]
    if reverse:
        in_specs += [pl.BlockSpec((L, 512), lambda b, c: (rows(b, c), 0)), pspec(OFF_CG), pl.BlockSpec((1, C_DV), cst2)]
        args += [o_fwd, proj, gn]
    return pl.pallas_call(
        functools.partial(_hgrn_kernel, L=L, nl=nl, reverse=reverse),
        grid=(B, nc),
        in_specs=in_specs,
        out_specs=pl.BlockSpec((L, 512), lambda b, c: (rows(b, c), 0)),
        out_shape=jax.ShapeDtypeStruct((T, 512), BF16 if reverse else F32),
        scratch_shapes=[pltpu.VMEM((C_HEADS, C_DV, C_DK), F32)],
        compiler_params=_cparams(("parallel", "arbitrary")),
        name="hgrn_bwd" if reverse else "hgrn_fwd",
    )(*args)


def _fft_consts(S, B):
    N2 = FFT_N2
    N1 = S // N2
    nbat = math.gcd(B, max(1, FFT1_ROWS // N1))
    d = np.arange(D_GROUP_DIM)
    ang = 2.0 * np.pi * ((d[:, None] * d[None, :]) % D_GROUP_DIM) / D_GROUP_DIM
    wc = np.concatenate([np.cos(ang), -np.sin(ang)], axis=1) / math.sqrt(D_GROUP_DIM)
    a = np.arange(N1)
    ang1 = 2.0 * np.pi * ((a[:, None] * a[None, :]) % N1) / N1
    eye = np.eye(nbat)
    c1, s1 = np.kron(eye, np.cos(ang1)), np.kron(eye, np.sin(ang1))
    m1 = np.block([[c1, s1], [-s1, c1]])
    p1 = jnp.arange(N1, dtype=jnp.int32)[:, None, None]
    p2 = jnp.arange(N2, dtype=jnp.int32)[None, :, None]
    s2 = jnp.arange(N2, dtype=jnp.int32)[None, None, :]
    kk = (s2 * (p1 + N1 * p2)) % S
    th = kk.astype(F32) * (2.0 * math.pi / S)
    m2 = jnp.concatenate([jnp.cos(th), jnp.sin(th)], axis=2) * (1.0 / math.sqrt(S))
    return jnp.asarray(wc, BF16), jnp.asarray(m1, BF16), m2.astype(BF16), N1, nbat


def _fft1_kernel(u_ref, wc_ref, m1_ref, ar_ref, ai_ref, *, rows, ct):
    wc, m1 = wc_ref[...], m1_ref[...]
    for g in range(ct // LANES):
        sl = slice(g * LANES, (g + 1) * LANES)
        z = _mm(u_ref[:, sl], wc)
        zs = jnp.concatenate([z[:, :LANES], z[:, LANES:]], axis=0).astype(BF16)
        a = _mm(m1, zs)
        ar_ref[:, sl] = a[:rows].astype(BF16)
        ai_ref[:, sl] = a[rows:].astype(BF16)


def _fft2_kernel(ar_ref, ai_ref, m2_ref, o_ref):
    for p in range(FFT2_P):
        x = jnp.concatenate([ar_ref[p], ai_ref[p]], axis=0)
        o_ref[:, p * 512:(p + 1) * 512] = _mm(m2_ref[p], x).astype(BF16)


def _fourier(u, consts, B, S):
    wc, m1, m2, N1, nbat = consts
    N2 = FFT_N2
    ct = FFT1_COLS
    P = FFT2_P
    W = N2 * 512
    rows = nbat * N1
    u2 = u.reshape(B * N1, W)
    sds = jax.ShapeDtypeStruct((B * N1, W), BF16)
    ar, ai = pl.pallas_call(
        functools.partial(_fft1_kernel, rows=rows, ct=ct),
        grid=(B // nbat, W // ct),
        in_specs=[
            pl.BlockSpec((rows, ct), lambda b, j: (b, j)),
            pl.BlockSpec(wc.shape, lambda b, j: (0, 0)),
            pl.BlockSpec(m1.shape, lambda b, j: (0, 0)),
        ],
        out_specs=[pl.BlockSpec((rows, ct), lambda b, j: (b, j))] * 2,
        out_shape=[sds, sds],
        compiler_params=_cparams(("parallel", "parallel")),
        name="fft_stage1",
    )(u2, wc, m1)
    ar3 = ar.reshape(B * N1, N2, 512)
    ai3 = ai.reshape(B * N1, N2, 512)
    y = pl.pallas_call(
        _fft2_kernel,
        grid=(B, N1 // P),
        in_specs=[
            pl.BlockSpec((P, N2, 512), lambda b, p: (b * (N1 // P) + p, 0, 0)),
            pl.BlockSpec((P, N2, 512), lambda b, p: (b * (N1 // P) + p, 0, 0)),
            pl.BlockSpec((P, N2, 2 * N2), lambda b, p: (p, 0, 0)),
        ],
        out_specs=pl.BlockSpec((N2, P * 512), lambda b, p: (b, p)),
        out_shape=jax.ShapeDtypeStruct((B * N2, N1 * 512), BF16),
        compiler_params=_cparams(("parallel", "parallel")),
        name="fft_stage2",
    )(ar3, ai3, m2)
    return y.reshape(B * S, 512)


def _merge_kernel(x_ref, oa_ref, ob_ref, oc_ref, od_ref, g1_ref, wg_ref, wb_ref, wo_ref, g2_ref, wr_ref,
                  x1_ref, h2_ref, aff_ref):
    x = x_ref[...]
    h = _rms(x, g1_ref[...]).astype(BF16)
    branches = (oa_ref, ob_ref, oc_ref, od_ref)
    mixed = None
    for n in range(N_BRANCHES):
        term = jax.nn.sigmoid(_mm(h, wg_ref[n])) * _mm(branches[n][...], wb_ref[n])
        mixed = term if mixed is None else mixed + term
    x1 = x + _mm(mixed.astype(BF16), wo_ref[...])
    x1_ref[...] = x1
    h2 = _rms(x1, g2_ref[...]).astype(BF16)
    h2_ref[...] = h2
    logits = _nt(wr_ref[...], h2)
    mx = jnp.max(logits, axis=0, keepdims=True)
    ex = jnp.exp(logits - mx)
    aff_ref[...] = ex / jnp.sum(ex, axis=0, keepdims=True)


def _merge(x2, oa, ob, oc, od, g1, wg, wb, wo, g2, wr_t):
    T = x2.shape[0]
    tm = TOKEN_TILE
    tok = lambda w: pl.BlockSpec((tm, w), lambda i: (i, 0))
    c2 = lambda i: (0, 0)
    c3 = lambda i: (0, 0, 0)
    return pl.pallas_call(
        _merge_kernel,
        grid=(T // tm,),
        in_specs=[
            tok(D_MODEL), tok(512), tok(512), tok(512), tok(512),
            pl.BlockSpec((1, D_MODEL), c2),
            pl.BlockSpec((N_BRANCHES, D_MODEL, D_MODEL), c3),
            pl.BlockSpec((N_BRANCHES, BRANCH_WIDTH, D_MODEL), c3),
            pl.BlockSpec((D_MODEL, D_MODEL), c2),
            pl.BlockSpec((1, D_MODEL), c2),
            pl.BlockSpec((N_EXPERTS, D_MODEL), c2),
        ],
        out_specs=[tok(D_MODEL), tok(D_MODEL), pl.BlockSpec((N_EXPERTS, tm), lambda i: (0, i))],
        out_shape=[jax.ShapeDtypeStruct((T, D_MODEL), F32), jax.ShapeDtypeStruct((T, D_MODEL), BF16),
                   jax.ShapeDtypeStruct((N_EXPERTS, T), F32)],
        compiler_params=_cparams(("parallel",)),
        name="merge",
    )(x2, oa, ob, oc, od, g1, wg, wb, wo, g2, wr_t)


def _cap_rows(T):
    cap = CAPACITY_FACTOR * T // N_EXPERTS
    ntiles = T // MOE_TILE
    need = cap + BF16_SUBLANES * ntiles + max(FFN_ROWS, MOE_SUB)
    return cap, ntiles, -(-need // FFN_ROWS) * FFN_ROWS


def _route_kernel(aff_ref, tri_ref, rank_ref, cnt_ref, off_ref, *, T, cap, ntiles):
    tk = MOE_TILE
    bits = pltpu.bitcast(aff_ref[...], jnp.int32)
    capf = jnp.float32(cap)

    def count(pred):
        return jnp.sum(jnp.where(pred, 1.0, 0.0), axis=-1, keepdims=True)

    def vbody(i, v):
        cand = v | lax.shift_left(jnp.int32(1), 30 - i)
        return jnp.where(count(bits >= cand) >= capf, cand, v)

    v = lax.fori_loop(0, 31, vbody, jnp.zeros((N_EXPERTS, 1), jnp.int32))
    need = capf - count(bits > v)
    ties = bits == v
    idx = lax.broadcasted_iota(jnp.int32, (N_EXPERTS, T), 1)
    nbits = max(1, (T - 1).bit_length())

    def jbody(i, j0):
        cand = j0 | lax.shift_left(jnp.int32(1), nbits - 1 - i)
        return jnp.where(count(ties & (idx < cand)) < need, cand, j0)

    j0 = lax.fori_loop(0, nbits, jbody, jnp.zeros((N_EXPERTS, 1), jnp.int32))
    sel = (bits > v) | (ties & (idx <= j0))
    tri = tri_ref[...]
    cnt_ref[...] = jnp.zeros_like(cnt_ref)
    for i in range(ntiles):
        sl = slice(i * tk, (i + 1) * tk)
        s_i = sel[:, sl]
        sf = jnp.where(s_i, 1.0, 0.0)
        r = _mm(sf.astype(BF16), tri)
        rank_ref[:, sl] = jnp.where(s_i, r, -1.0)
        cnt_ref[:, i:i + 1] = jnp.sum(sf, axis=-1, keepdims=True)
    cnt = cnt_ref[...]
    units = jnp.ceil(cnt * (1.0 / BF16_SUBLANES))
    off_ref[...] = (_mm(units.astype(BF16), tri[:LANES, :LANES]) * BF16_SUBLANES).astype(jnp.int32)


def _route(aff_t, tri, T):
    cap, ntiles, _ = _cap_rows(T)
    assert ntiles <= LANES and MOE_TILE // BF16_SUBLANES <= 256
    full = lambda s: pl.BlockSpec(s, lambda: tuple(0 for _ in s))
    rank, cnt, off = pl.pallas_call(
        functools.partial(_route_kernel, T=T, cap=cap, ntiles=ntiles),
        in_specs=[full((N_EXPERTS, T)), full(tri.shape)],
        out_specs=[full((N_EXPERTS, T)), full((N_EXPERTS, LANES)), full((N_EXPERTS, LANES))],
        out_shape=[jax.ShapeDtypeStruct((N_EXPERTS, T), F32), jax.ShapeDtypeStruct((N_EXPERTS, LANES), F32),
                   jax.ShapeDtypeStruct((N_EXPERTS, LANES), jnp.int32)],
        compiler_params=_cparams(()),
        name="moe_route",
    )(aff_t, tri)
    return rank, cnt.astype(jnp.int32), off


def _gather_kernel(cnt_ref, off_ref, rank_ref, h_ref, xe_ref, stage_ref, extra_ref, zero_ref, sem, xsem, *, cap):
    i, n = pl.program_id(0), pl.num_programs(0)
    R = MOE_SUB
    slot = i % 2
    h = h_ref[...]
    jrow = lax.broadcasted_iota(jnp.int32, (R, MOE_TILE), 0).astype(F32)
    zrows = zero_ref.shape[0]
    nzero = (xe_ref.shape[1] - cap) // zrows

    @pl.when(i == 0)
    def _():
        zero_ref[...] = jnp.zeros_like(zero_ref)

        def zero_copy(e, k):
            return pltpu.make_async_copy(zero_ref, xe_ref.at[e, pl.ds(cap + k * zrows, zrows)], sem.at[0, e])

        for k in range(nzero):
            for e in range(N_EXPERTS):
                zero_copy(e, k).start()
            for e in range(N_EXPERTS):
                zero_copy(e, k).wait()

    def rows(e, r):
        hit = rank_ref[e:e + 1, :] == jrow + (r * R).astype(F32)
        return _mm(jnp.where(hit, 1.0, 0.0).astype(BF16), h).astype(BF16)

    def out_copy(step, sl, e):
        r0 = pl.multiple_of(off_ref[e, step], BF16_SUBLANES)
        return pltpu.make_async_copy(stage_ref.at[sl, e], xe_ref.at[e, pl.ds(r0, R)], sem.at[sl, e])

    for e in range(N_EXPERTS):
        stage_ref[slot, e] = rows(e, jnp.int32(0))

    @pl.when(i > 0)
    def _():
        for e in range(N_EXPERTS):
            out_copy(i - 1, 1 - slot, e).wait()

    for e in range(N_EXPERTS):
        out_copy(i, slot, e).start()

    for e in range(N_EXPERTS):
        def body(r, carry, e=e):
            extra_ref[...] = rows(e, r)
            r0 = pl.multiple_of(off_ref[e, i] + r * R, BF16_SUBLANES)
            cp = pltpu.make_async_copy(extra_ref, xe_ref.at[e, pl.ds(r0, R)], xsem.at[0])
            cp.start()
            cp.wait()
            return carry

        lax.fori_loop(1, (cnt_ref[e, i] + R - 1) // R, body, 0)

    @pl.when(i == n - 1)
    def _():
        for e in range(N_EXPERTS):
            out_copy(i, slot, e).wait()


def _gather(cnt, off, rank, h2, T):
    cap, ntiles, rows = _cap_rows(T)
    tk = MOE_TILE
    R = MOE_SUB
    return pl.pallas_call(
        functools.partial(_gather_kernel, cap=cap),
        grid_spec=pltpu.PrefetchScalarGridSpec(
            num_scalar_prefetch=2,
            grid=(ntiles,),
            in_specs=[
                pl.BlockSpec((N_EXPERTS, tk), lambda i, c, o: (0, i)),
                pl.BlockSpec((tk, D_MODEL), lambda i, c, o: (i, 0)),
            ],
            out_specs=pl.BlockSpec(memory_space=pl.ANY),
            scratch_shapes=[
                pltpu.VMEM((2, N_EXPERTS, R, D_MODEL), BF16),
                pltpu.VMEM((R, D_MODEL), BF16),
                pltpu.VMEM((math.gcd(rows - cap, FFN_ROWS), D_MODEL), BF16),
                pltpu.SemaphoreType.DMA((2, N_EXPERTS)),
                pltpu.SemaphoreType.DMA((1,)),
            ],
        ),
        out_shape=jax.ShapeDtypeStruct((N_EXPERTS, rows, D_MODEL), BF16),
        compiler_params=_cparams(("arbitrary",)),
        name="moe_gather",
    )(cnt, off, rank, h2)


def _ffn_kernel(used_ref, x_ref, wg_ref, wu_ref, wd_ref, y_ref):
    live = pl.program_id(1) * FFN_ROWS < used_ref[pl.program_id(0)]

    @pl.when(live)
    def _():
        x = x_ref[0]
        a = _mm(x, wg_ref[0])
        u = _mm(x, wu_ref[0])
        hmid = (a * jax.nn.sigmoid(a) * u).astype(BF16)
        y_ref[0] = _mm(hmid, wd_ref[0]).astype(BF16)

    @pl.when(jnp.logical_not(live))
    def _():
        y_ref[...] = jnp.zeros_like(y_ref)


def _ffn(used, xe, wg, wu, wd):
    E, rows, _ = xe.shape
    wspec = lambda s: pl.BlockSpec((1,) + s, lambda e, j, u: (e, 0, 0))
    return pl.pallas_call(
        _ffn_kernel,
        grid_spec=pltpu.PrefetchScalarGridSpec(
            num_scalar_prefetch=1,
            grid=(E, rows // FFN_ROWS),
            in_specs=[
                pl.BlockSpec((1, FFN_ROWS, D_MODEL), lambda e, j, u: (e, j, 0)),
                wspec((D_MODEL, EXPERT_FF)), wspec((D_MODEL, EXPERT_FF)), wspec((EXPERT_FF, D_MODEL)),
            ],
            out_specs=pl.BlockSpec((1, FFN_ROWS, D_MODEL), lambda e, j, u: (e, j, 0)),
        ),
        out_shape=jax.ShapeDtypeStruct(xe.shape, BF16),
        compiler_params=_cparams(("parallel", "parallel")),
        name="moe_ffn",
    )(used, xe, wg, wu, wd)


def _scatter_kernel(cnt_ref, off_ref, x_ref, rank_ref, aff_ref, ye_ref, o_ref, seg_ref, oh_ref, extra_ref, sem, xsem):
    i, n = pl.program_id(0), pl.num_programs(0)
    R = MOE_SUB
    tk = MOE_TILE
    slot = i % 2

    def seg_copy(step, sl, e):
        r0 = pl.multiple_of(off_ref[e, step], BF16_SUBLANES)
        return pltpu.make_async_copy(ye_ref.at[e, pl.ds(r0, R)], seg_ref.at[sl, pl.ds(e * R, R)], sem.at[sl, e])

    @pl.when(i == 0)
    def _():
        for e in range(N_EXPERTS):
            seg_copy(i, slot, e).start()

    @pl.when(i + 1 < n)
    def _():
        for e in range(N_EXPERTS):
            seg_copy(i + 1, 1 - slot, e).start()

    jlane = lax.broadcasted_iota(jnp.int32, (tk, R), 1).astype(F32)
    for e in range(N_EXPERTS):
        oh_ref[:, e * R:(e + 1) * R] = jnp.where(rank_ref[:, e:e + 1] == jlane, aff_ref[:, e:e + 1], 0.0).astype(BF16)
    for e in range(N_EXPERTS):
        seg_copy(i, slot, e).wait()
    o_ref[...] = x_ref[...] + _mm(oh_ref[...], seg_ref[slot])

    for e in range(N_EXPERTS):
        def body(r, carry, e=e):
            r0 = pl.multiple_of(off_ref[e, i] + r * R, BF16_SUBLANES)
            cp = pltpu.make_async_copy(ye_ref.at[e, pl.ds(r0, R)], extra_ref, xsem.at[0])
            cp.start()
            hit = rank_ref[:, e:e + 1] == jlane + (r * R).astype(F32)
            w = jnp.where(hit, aff_ref[:, e:e + 1], 0.0).astype(BF16)
            cp.wait()
            o_ref[...] += _mm(w, extra_ref[...])
            return carry

        lax.fori_loop(1, (cnt_ref[e, i] + R - 1) // R, body, 0)


def _scatter(cnt, off, x1, rank_t, aff_tok, ye, T):
    _, ntiles, rows = _cap_rows(T)
    tk = MOE_TILE
    R = MOE_SUB
    return pl.pallas_call(
        _scatter_kernel,
        grid_spec=pltpu.PrefetchScalarGridSpec(
            num_scalar_prefetch=2,
            grid=(ntiles,),
            in_specs=[
                pl.BlockSpec((tk, D_MODEL), lambda i, c, o: (i, 0)),
                pl.BlockSpec((tk, N_EXPERTS), lambda i, c, o: (i, 0)),
                pl.BlockSpec((tk, N_EXPERTS), lambda i, c, o: (i, 0)),
                pl.BlockSpec(memory_space=pl.ANY),
            ],
            out_specs=pl.BlockSpec((tk, D_MODEL), lambda i, c, o: (i, 0)),
            scratch_shapes=[
                pltpu.VMEM((2, N_EXPERTS * R, D_MODEL), BF16),
                pltpu.VMEM((tk, N_EXPERTS * R), BF16),
                pltpu.VMEM((R, D_MODEL), BF16),
                pltpu.SemaphoreType.DMA((2, N_EXPERTS)),
                pltpu.SemaphoreType.DMA((1,)),
            ],
        ),
        out_shape=jax.ShapeDtypeStruct((T, D_MODEL), F32),
        compiler_params=_cparams(("arbitrary",)),
        name="moe_scatter",
    )(cnt, off, x1, rank_t, aff_tok, ye)


def _expert_choice(x1, h2, aff_t, tri, wg, wu, wd):
    T = x1.shape[0]
    rank, cnt, off = _route(aff_t, tri, T)
    ntiles = T // MOE_TILE
    used = off[:, ntiles - 1] + -(-cnt[:, ntiles - 1] // BF16_SUBLANES) * BF16_SUBLANES
    xe = _gather(cnt, off, rank, h2, T)
    ye = _ffn(used, xe, wg, wu, wd)
    return _scatter(cnt, off, x1, rank.T, aff_t.T, ye, T)


def _pad_heads(w, n_heads, width, pad_to):
    k = w.shape[0]
    w = w.reshape(k, n_heads, width)
    return jnp.pad(w, ((0, 0), (0, 0), (0, pad_to - width))).reshape(k, n_heads * pad_to)


def _pack_layer(l, p):
    w_in = p['w_in'][l]
    edges = np.cumsum([0, 512, 128, 128, B_Q_LORA, B_KV_LORA + B_ROPE, 512, 512, 512, 512, 512, 512])
    a_q, a_k, a_v, b_cq, b_ckv, c_q, c_ff, c_fb, c_i, c_g, d_u = [w_in[:, edges[j]:edges[j + 1]] for j in range(11)]
    zeros = lambda n: jnp.zeros((D_MODEL, n), w_in.dtype)
    b_kpe = jnp.concatenate([zeros(B_NOPE), b_ckv[:, B_KV_LORA:], zeros(PAD_HEAD - B_QK)], axis=1)
    w_main = jnp.concatenate([a_q, c_q, c_ff, c_fb, c_i, c_g, a_k, a_v, b_cq, b_ckv[:, :B_KV_LORA], b_kpe,
                              _swap_rope_halves(b_kpe)], axis=1).astype(BF16)
    kv_up = p['b_kv_up'][l].reshape(B_KV_LORA, B_HEADS, B_NOPE + B_V)
    pad_g = lambda g: jnp.pad(g, (0, PAD_HEAD - B_QK)).reshape(1, PAD_HEAD)
    b_wq = _pad_heads(p['b_q_up'][l], B_HEADS, B_QK, PAD_HEAD)
    b_wqsw = _swap_rope_halves(b_wq.reshape(B_Q_LORA, B_HEADS, PAD_HEAD)).reshape(B_Q_LORA, B_HEADS * PAD_HEAD)
    b_gq, b_gk = pad_g(p['b_q_norm_g'][l]), pad_g(p['b_k_norm_g'][l])
    sink = p['a_sink'][l].astype(F32).reshape(A_KV_HEADS, A_HEADS // A_KV_HEADS)
    b_bound = math.sqrt(B_QK) * jnp.max(jnp.abs(p['b_q_norm_g'][l])) * jnp.max(jnp.abs(p['b_k_norm_g'][l]))
    lane = jnp.arange(PAD_HEAD)
    b_qadd = jnp.where(lane == SCORE_LANE, 1.0, 0.0).astype(F32).reshape(1, PAD_HEAD)
    b_kadd = jnp.where(lane == SCORE_LANE, -LOG2E * b_bound, 0.0).astype(F32).reshape(1, PAD_HEAD)
    b_vadd = jnp.tile(jnp.where(lane == SUM_LANE, 1.0, 0.0).astype(F32), B_HEADS).reshape(1, B_HEADS * PAD_HEAD)
    a_bound, a_bias, a_sink_term = _win_fast_tables(p['a_q_norm_g'][l], p['a_k_norm_g'][l], p['a_sink'][l])
    return dict(
        a_bound=a_bound, a_bias=a_bias, a_sink_term=a_sink_term,
        a_gq_t=(jnp.tile(p['a_q_norm_g'][l], A_HEADS) * (LOG2E * A_HEAD_DIM ** -0.5)).reshape(1, A_HEADS * A_HEAD_DIM),
        a_gk_t=jnp.tile(p['a_k_norm_g'][l], A_KV_HEADS).reshape(1, A_KV_HEADS * A_HEAD_DIM),
        b_bound=b_bound, b_adds=(b_qadd, b_kadd, b_vadd),
        g_mix=p['norm_mix_g'][l].reshape(1, D_MODEL), w_main=w_main, w_u=d_u.astype(BF16),
        a_gq=p['a_q_norm_g'][l].reshape(1, A_HEAD_DIM), a_gk=p['a_k_norm_g'][l].reshape(1, A_HEAD_DIM),
        a_sink=jnp.repeat(sink, WINDOW, axis=1)[:, :, None],
        b_gqa=p['b_q_a_norm_g'][l].reshape(1, B_Q_LORA), b_gkva=p['b_kv_a_norm_g'][l].reshape(1, B_KV_LORA),
        b_wq=b_wq.astype(BF16), b_wqsw=b_wqsw.astype(BF16),
        b_wk=_pad_heads(kv_up[:, :, :B_NOPE].reshape(B_KV_LORA, -1), B_HEADS, B_NOPE, PAD_HEAD).astype(BF16),
        b_wv=_pad_heads(kv_up[:, :, B_NOPE:].reshape(B_KV_LORA, -1), B_HEADS, B_V, PAD_HEAD).astype(BF16),
        b_gq=b_gq, b_gk=b_gk, b_gqsw=_swap_rope_halves(b_gq), b_gksw=_swap_rope_halves(b_gk),
        c_gn=p['c_out_norm_g'][l].reshape(1, C_DV),
        w_gate=p['w_gate'][l].astype(BF16), w_branch=p['w_branch'][l].astype(BF16), w_out=p['w_out'][l].astype(BF16),
        g_ffn=p['norm_ffn_g'][l].reshape(1, D_MODEL), w_router_t=p['w_router'][l].T.astype(BF16),
        w_e_gate=p['w_e_gate'][l].astype(BF16), w_e_up=p['w_e_up'][l].astype(BF16), w_e_down=p['w_e_down'][l].astype(BF16),
    )


def _layer(x2, B, S, lp, lb, cst):
    proj, u = _in_proj(x2, lp['g_mix'], lp['w_main'], lp['w_u'])
    o_a = lax.cond(lp['a_bound'] <= FAST_SOFTMAX_MAX_BOUND,
                   lambda pr: _win_attn_fast(pr, lp['a_gq_t'], lp['a_gk_t'], cst['win'], lp['a_bias'], lp['a_sink_term'], B, S),
                   lambda pr: _win_attn(pr, lp['a_gq'], lp['a_gk'], lp['a_sink'], B, S), proj)
    q, k, v = _mla_prep(proj, lp, cst['rope'][S], S)
    o_b = lax.cond(lp['b_bound'] <= FAST_SOFTMAX_MAX_BOUND,
                   lambda q, k, v: _flash(q, k, v, B, S, False),
                   lambda q, k, v: _flash(q, k, v, B, S, True), q, k, v)
    o_f = _hgrn(proj, lb[0:1], cst['hgrn_f'], B, S, False)
    o_c = _hgrn(proj, lb[1:2], cst['hgrn_b'], B, S, True, o_f, lp['c_gn'])
    o_d = _fourier(u, cst['fft'][S], B, S)
    x1, h2, aff_t = _merge(x2, o_a, o_b, o_c, o_d, lp['g_mix'], lp['w_gate'], lp['w_branch'], lp['w_out'],
                           lp['g_ffn'], lp['w_router_t'])
    return _expert_choice(x1, h2, aff_t, cst['tri'], lp['w_e_gate'], lp['w_e_up'], lp['w_e_down'])


def kernel(x_prompt, x_sample, norm_mix_g, w_in, a_q_norm_g, a_k_norm_g, a_sink, b_q_a_norm_g, b_q_up,
           b_kv_a_norm_g, b_kv_up, b_q_norm_g, b_k_norm_g, c_lb_logits, c_out_norm_g, w_branch, w_gate, w_out,
           norm_ffn_g, w_router, w_e_gate, w_e_up, w_e_down):
    p = dict(norm_mix_g=norm_mix_g, w_in=w_in, a_q_norm_g=a_q_norm_g, a_k_norm_g=a_k_norm_g, a_sink=a_sink,
             b_q_a_norm_g=b_q_a_norm_g, b_q_up=b_q_up, b_kv_a_norm_g=b_kv_a_norm_g, b_kv_up=b_kv_up,
             b_q_norm_g=b_q_norm_g, b_k_norm_g=b_k_norm_g, c_out_norm_g=c_out_norm_g, w_branch=w_branch,
             w_gate=w_gate, w_out=w_out, norm_ffn_g=norm_ffn_g, w_router=w_router, w_e_gate=w_e_gate,
             w_e_up=w_e_up, w_e_down=w_e_down)
    lb_w = jax.nn.softmax(c_lb_logits.astype(F32), axis=0)
    lb_all = jnp.cumsum(lb_w, axis=0) - lb_w[0:1]
    groups = [x_prompt, x_sample]
    seqs = sorted({g.shape[1] for g in groups})
    tk = MOE_TILE
    cst = dict(
        rope={S: _rope_tables(S) for S in seqs},
        fft={g.shape[1]: _fft_consts(g.shape[1], g.shape[0]) for g in groups},
        hgrn_f=_hgrn_consts(HGRN_CHUNK, False), hgrn_b=_hgrn_consts(HGRN_CHUNK, True),
        win=_win_fast_consts(),
        tri=jnp.asarray(np.triu(np.ones((tk, tk), np.float32), 1), BF16),
    )
    layers = [_pack_layer(l, p) for l in range(DEPTH)]
    outs = []
    for xg in groups:
        B, S, _ = xg.shape
        y = xg.reshape(B * S, D_MODEL)
        for l in range(DEPTH):
            y = _layer(y, B, S, layers[l], lb_all[l], cst)
        outs.append(y.reshape(B, S, D_MODEL))
    return tuple(outs)
```

```python
import functools
import math

import numpy as np
import jax
import jax.numpy as jnp
from jax import lax
from jax.experimental import pallas as pl
from jax.experimental.pallas import tpu as pltpu

F32 = jnp.float32
BF16 = jnp.bfloat16

D_MODEL = 1024
DEPTH = 2
A_HEADS, A_KV_HEADS, A_HEAD_DIM, WINDOW = 8, 2, 64, 128
B_HEADS, B_Q_LORA, B_KV_LORA, B_NOPE, B_ROPE, B_V = 8, 256, 128, 64, 32, 64
B_QK = B_NOPE + B_ROPE
ROPE_THETA = 10000.0
C_HEADS, C_DK, C_DV = 4, 128, 128
D_GROUPS, D_GROUP_DIM = 4, 128
N_BRANCHES, BRANCH_WIDTH = 4, 512
N_EXPERTS, EXPERT_FF, CAPACITY_FACTOR = 16, 1024, 2
EPS = 1e-6
MASK_VALUE = -1e30
TINY = 1e-30
LOG2E = 1.4426950408889634

SCORE_LANE = B_QK
SUM_LANE = B_V
FAST_SOFTMAX_MAX_BOUND = 30.0

LANES = 128
BF16_SUBLANES = 16
V7X_VMEM_LIMIT = 56 * 1024 * 1024

TOKEN_TILE = 512
PAD_HEAD = 128
HGRN_CHUNK = 128
FFT_N2 = 128
FFT1_COLS = 4096
FFT1_ROWS = 64
FFT2_P = 4
FLASH_TQ, FLASH_TK, FLASH_UNROLL = 512, 512, 5
MOE_TILE = 512
MOE_SUB = 128
FFN_ROWS = 512

OFF_AQ, OFF_CQ, OFF_CFF, OFF_CFB, OFF_CI, OFF_CG = 0, 512, 1024, 1536, 2048, 2560
OFF_AK, OFF_AV, OFF_BCQ, OFF_BLAT, OFF_BKPE, OFF_BKPESW = 3072, 3200, 3328, 3584, 3712, 3840
PROJ_W = 3968


def _cparams(sem, vmem=V7X_VMEM_LIMIT):
    return pltpu.CompilerParams(dimension_semantics=sem, vmem_limit_bytes=vmem)


def _nt(a, b):
    return lax.dot_general(a, b, (((1,), (1,)), ((), ())), preferred_element_type=F32)


def _mm(a, b):
    return jnp.dot(a, b, preferred_element_type=F32)


def _rms(x, g):
    return x * lax.rsqrt(jnp.mean(x * x, axis=-1, keepdims=True) + EPS) * g


def _in_proj_kernel(x_ref, g_ref, w_ref, wu_ref, p_ref, u_ref):
    h = _rms(x_ref[...], g_ref[...]).astype(BF16)
    for c0 in range(0, PROJ_W, 512):
        c1 = min(c0 + 512, PROJ_W)
        p_ref[:, c0:c1] = _mm(h, w_ref[:, c0:c1]).astype(BF16)
    u_ref[...] = _mm(h, wu_ref[...]).astype(BF16)


def _in_proj(x2, g, w_main, w_u):
    T = x2.shape[0]
    tm = TOKEN_TILE
    return pl.pallas_call(
        _in_proj_kernel,
        grid=(T // tm,),
        in_specs=[
            pl.BlockSpec((tm, D_MODEL), lambda i: (i, 0)),
            pl.BlockSpec((1, D_MODEL), lambda i: (0, 0)),
            pl.BlockSpec((D_MODEL, PROJ_W), lambda i: (0, 0)),
            pl.BlockSpec((D_MODEL, 512), lambda i: (0, 0)),
        ],
        out_specs=[
            pl.BlockSpec((tm, PROJ_W), lambda i: (i, 0)),
            pl.BlockSpec((tm, 512), lambda i: (i, 0)),
        ],
        out_shape=[jax.ShapeDtypeStruct((T, PROJ_W), BF16), jax.ShapeDtypeStruct((T, 512), BF16)],
        compiler_params=_cparams(("parallel",)),
        name="in_proj",
    )(x2, g, w_main, w_u)


def _win_attn_kernel(q_ref, kp_ref, kc_ref, kn_ref, vp_ref, vc_ref, vn_ref, gq_ref, gk_ref, sink_ref, o_ref, *, nb):
    n = pl.program_id(1)
    W = WINDOW
    G = A_HEADS // A_KV_HEADS
    q = q_ref[...].astype(F32)
    kcat = jnp.concatenate([kp_ref[...], kc_ref[...], kn_ref[...]], axis=0).astype(F32)
    vcat = jnp.concatenate([vp_ref[...], vc_ref[...], vn_ref[...]], axis=0)
    gq = gq_ref[...]
    gk = gk_ref[...]
    row = lax.broadcasted_iota(jnp.int32, (G * W, 3 * W), 0)
    col = lax.broadcasted_iota(jnp.int32, (G * W, 3 * W), 1)
    qi = row % W
    dist = jnp.abs(col - W - qi)
    valid = dist <= W
    valid = valid & ((col >= W) | (n > 0)) & ((col < 2 * W) | (n < nb - 1))
    distf = dist.astype(F32)
    hrow = lax.broadcasted_iota(jnp.int32, (G * W, 1), 0) // W
    for g in range(A_KV_HEADS):
        qs = []
        for j in range(G):
            h = g * G + j
            qs.append(_rms(q[:, h * A_HEAD_DIM:(h + 1) * A_HEAD_DIM], gq))
        qg = (jnp.concatenate(qs, axis=0) * (A_HEAD_DIM ** -0.5)).astype(BF16)
        kg = _rms(kcat[:, g * A_HEAD_DIM:(g + 1) * A_HEAD_DIM], gk).astype(BF16)
        vg = vcat[:, g * A_HEAD_DIM:(g + 1) * A_HEAD_DIM]
        slope = jnp.exp2(-(hrow + (g * G + 1)).astype(F32))
        s = _nt(qg, kg) - slope * distf
        s = jnp.where(valid, s, MASK_VALUE)
        sk = sink_ref[g]
        m = jnp.maximum(jnp.max(s, axis=-1, keepdims=True), sk)
        p = jnp.exp(s - m)
        den = jnp.sum(p, axis=-1, keepdims=True) + jnp.exp(sk - m)
        o = _mm(p.astype(BF16), vg) / den
        for j in range(G):
            h = g * G + j
            o_ref[:, h * A_HEAD_DIM:(h + 1) * A_HEAD_DIM] = o[j * W:(j + 1) * W].astype(BF16)


def _win_fast_consts():
    d = A_HEAD_DIM
    lane = np.arange(A_HEADS * d)
    return jnp.asarray((lane[:, None] // d == lane[None, :] // d).astype(np.float32) / d, BF16)


def _win_fast_tables(gq, gk, sink):
    W = WINDOW
    bound = math.sqrt(A_HEAD_DIM) * jnp.max(jnp.abs(gq)) * jnp.max(jnp.abs(gk))
    sink = sink.astype(F32)
    m = jnp.maximum(bound, sink)
    slopes = 2.0 ** (-8.0 * jnp.arange(1, A_HEADS + 1, dtype=F32) / A_HEADS)
    r = jnp.arange(W)[:, None]
    c = jnp.arange(3 * W)[None, :]
    dist = jnp.abs(c - W - r)
    inside = dist <= W
    variants = []
    for first, last in ((False, False), (True, False), (False, True), (True, True)):
        valid = inside & ((c >= W) | (not first)) & ((c < 2 * W) | (not last))
        b = LOG2E * (-slopes[:, None, None] * dist[None].astype(F32) - m[:, None, None])
        variants.append(jnp.where(valid[None], b, MASK_VALUE))
    sink_term = jnp.exp2(LOG2E * (sink - m))
    return bound, jnp.stack(variants), jnp.broadcast_to(sink_term[:, None, None], (A_HEADS, 1, LANES))


def _win_fast_kernel(q_ref, kp_ref, kc_ref, kn_ref, vp_ref, vc_ref, vn_ref, gq_ref, gk_ref, bd_ref, bias_ref, st_ref,
                     o_ref):
    d = A_HEAD_DIM
    q = q_ref[...].astype(F32)
    bd = bd_ref[...]
    qf = q * lax.rsqrt(_mm((q * q).astype(BF16), bd) + EPS) * gq_ref[...]
    kcat = jnp.concatenate([kp_ref[...], kc_ref[...], kn_ref[...]], axis=0).astype(F32)
    kss = _mm((kcat * kcat).astype(BF16), bd[:LANES, :LANES])
    kn = kcat * lax.rsqrt(kss + EPS) * gk_ref[...]
    vcat = jnp.concatenate([vp_ref[...], vc_ref[...], vn_ref[...]], axis=0).astype(F32)
    ones = jnp.ones((3 * WINDOW, LANES), BF16)
    lane = lax.broadcasted_iota(jnp.int32, (1, LANES), 1)
    half = [jnp.where(lane < d, 1.0, 0.0), jnp.where(lane >= d, 1.0, 0.0)]
    kn_sw, v_sw = pltpu.roll(kn, d, 1), pltpu.roll(vcat, d, 1)
    G = A_HEADS // A_KV_HEADS
    for g in range(A_KV_HEADS):
        kk = jnp.where((lane < d) == (g == 0), kn, kn_sw).astype(BF16)
        rhs = [jnp.concatenate([((vcat if par == g else v_sw) * half[par]).astype(BF16), ones], axis=1)
               for par in range(2)]
        for jj in range(G // 2):
            pair = g * (G // 2) + jj
            qp = qf[:, pair * LANES:(pair + 1) * LANES]
            acc = None
            for par in range(2):
                h = 2 * pair + par
                s = _nt((qp * half[par]).astype(BF16), kk) + bias_ref[0, h]
                res = _mm(jnp.exp2(s).astype(BF16), rhs[par])
                o = res[:, :LANES] / (res[:, LANES:] + st_ref[h])
                acc = o if acc is None else acc + o
            o_ref[:, pair * LANES:(pair + 1) * LANES] = acc.astype(BF16)


def _win_attn_fast(proj, gq_t, gk_t, consts, bias, sink_term, B, S):
    T = B * S
    W = WINDOW
    nb = S // W
    kcol, vcol = OFF_AK // 128, OFF_AV // 128
    bd = consts
    cur = lambda b, n: b * nb + n
    prev = lambda b, n: b * nb + jnp.maximum(n - 1, 0)
    nxt = lambda b, n: b * nb + jnp.minimum(n + 1, nb - 1)
    spec = lambda rowf, c: pl.BlockSpec((W, 128), lambda b, n: (rowf(b, n), c))
    variant = lambda b, n: ((n == 0).astype(jnp.int32) + 2 * (n == nb - 1).astype(jnp.int32), 0, 0, 0)
    c2 = lambda b, n: (0, 0)
    c3 = lambda b, n: (0, 0, 0)
    return pl.pallas_call(
        _win_fast_kernel,
        grid=(B, nb),
        in_specs=[
            pl.BlockSpec((W, 512), lambda b, n: (cur(b, n), OFF_AQ // 512)),
            spec(prev, kcol), spec(cur, kcol), spec(nxt, kcol),
            spec(prev, vcol), spec(cur, vcol), spec(nxt, vcol),
            pl.BlockSpec((1, 512), c2), pl.BlockSpec((1, LANES), c2),
            pl.BlockSpec(bd.shape, c2),
            pl.BlockSpec((1, A_HEADS, W, 3 * W), variant),
            pl.BlockSpec((A_HEADS, 1, LANES), c3),
        ],
        out_specs=pl.BlockSpec((W, 512), lambda b, n: (cur(b, n), 0)),
        out_shape=jax.ShapeDtypeStruct((T, 512), BF16),
        compiler_params=_cparams(("parallel", "parallel")),
        name="win_attn_fast",
    )(proj, proj, proj, proj, proj, proj, proj, gq_t, gk_t, bd, bias, sink_term)


def _win_attn(proj, gq, gk, sink_cols, B, S):
    T = B * S
    W = WINDOW
    nb = S // W
    kcol, vcol = OFF_AK // 128, OFF_AV // 128

    def cur(b, n):
        return b * nb + n

    def prev(b, n):
        return b * nb + jnp.maximum(n - 1, 0)

    def nxt(b, n):
        return b * nb + jnp.minimum(n + 1, nb - 1)

    def spec(rowf, c):
        return pl.BlockSpec((W, 128), lambda b, n: (rowf(b, n), c))

    return pl.pallas_call(
        functools.partial(_win_attn_kernel, nb=nb),
        grid=(B, nb),
        in_specs=[
            pl.BlockSpec((W, 512), lambda b, n: (cur(b, n), OFF_AQ // 512)),
            spec(prev, kcol), spec(cur, kcol), spec(nxt, kcol),
            spec(prev, vcol), spec(cur, vcol), spec(nxt, vcol),
            pl.BlockSpec((1, A_HEAD_DIM), lambda b, n: (0, 0)),
            pl.BlockSpec((1, A_HEAD_DIM), lambda b, n: (0, 0)),
            pl.BlockSpec((A_KV_HEADS, 4 * W, 1), lambda b, n: (0, 0, 0)),
        ],
        out_specs=pl.BlockSpec((W, 512), lambda b, n: (cur(b, n), 0)),
        out_shape=jax.ShapeDtypeStruct((T, 512), BF16),
        compiler_params=_cparams(("parallel", "parallel")),
        name="win_attn",
    )(proj, proj, proj, proj, proj, proj, proj, gq, gk, sink_cols)


def _rope_tables(S):
    half = B_ROPE // 2
    inv = 1.0 / (ROPE_THETA ** (np.arange(half, dtype=np.float32) / half))
    ang = np.arange(S, dtype=np.float32)[:, None] * inv[None, :]
    cos, sin = np.cos(ang), np.sin(ang)
    cosf = np.ones((S, PAD_HEAD), np.float32)
    sinf = np.zeros((S, PAD_HEAD), np.float32)
    cosf[:, B_NOPE:B_NOPE + half] = cos
    cosf[:, B_NOPE + half:B_NOPE + 2 * half] = cos
    sinf[:, B_NOPE:B_NOPE + half] = -sin
    sinf[:, B_NOPE + half:B_NOPE + 2 * half] = sin
    return jnp.asarray(cosf), jnp.asarray(sinf)


def _swap_rope_halves(w):
    half = B_ROPE // 2
    z = jnp.zeros_like(w)
    z = z.at[..., B_NOPE:B_NOPE + half].set(w[..., B_NOPE + half:B_NOPE + 2 * half])
    return z.at[..., B_NOPE + half:B_NOPE + 2 * half].set(w[..., B_NOPE:B_NOPE + half])


def _mla_prep_kernel(cq_ref, lat_ref, kpe_ref, kpesw_ref, gqa_ref, gkva_ref, wq_ref, wqsw_ref, wk_ref, wv_ref,
                     gq_ref, gqsw_ref, gk_ref, gksw_ref, cos_ref, sin_ref, qadd_ref, kadd_ref, vadd_ref,
                     q_ref, k_ref, v_ref):
    cosf, sinf = cos_ref[...], sin_ref[...]
    qscale = LOG2E * B_QK ** -0.5
    g1q, g2q = gq_ref[...] * cosf * qscale, gqsw_ref[...] * sinf * qscale
    g1k, g2k = gk_ref[...] * cosf, gksw_ref[...] * sinf

    def inv_rms(x):
        return lax.rsqrt(jnp.sum(x * x, axis=-1, keepdims=True) * (1.0 / B_QK) + EPS)

    cq = _rms(cq_ref[...].astype(F32), gqa_ref[...]).astype(BF16)
    qall = _mm(cq, wq_ref[...])
    qsw = _mm(cq, wqsw_ref[...])
    c = _rms(lat_ref[...].astype(F32), gkva_ref[...]).astype(BF16)
    kall = _mm(c, wk_ref[...])
    v_ref[...] = (_mm(c, wv_ref[...]) + vadd_ref[...]).astype(BF16)
    kpe = kpe_ref[...].astype(F32)
    krot = kpesw_ref[...].astype(F32) * g2k
    qadd, kadd = qadd_ref[...], kadd_ref[...]
    for h in range(B_HEADS):
        sl = slice(h * PAD_HEAD, (h + 1) * PAD_HEAD)
        x = qall[:, sl]
        q_ref[:, sl] = ((x * g1q + qsw[:, sl] * g2q) * inv_rms(x) + qadd).astype(BF16)
        y = kall[:, sl] + kpe
        k_ref[:, sl] = ((y * g1k + krot) * inv_rms(y) + kadd).astype(BF16)


def _mla_prep(proj, lp, tables, S):
    T = proj.shape[0]
    tm = TOKEN_TILE
    nper = S // tm
    HW = B_HEADS * PAD_HEAD
    cst = lambda i: (0, 0)
    tab = pl.BlockSpec((tm, PAD_HEAD), lambda i: (i % nper, 0))
    row = pl.BlockSpec((1, PAD_HEAD), cst)
    col = lambda off: pl.BlockSpec((tm, PAD_HEAD), lambda i: (i, off // PAD_HEAD))
    out = jax.ShapeDtypeStruct((T, HW), BF16)
    return pl.pallas_call(
        _mla_prep_kernel,
        grid=(T // tm,),
        in_specs=[
            pl.BlockSpec((tm, 256), lambda i: (i, OFF_BCQ // 256)),
            col(OFF_BLAT), col(OFF_BKPE), col(OFF_BKPESW),
            pl.BlockSpec((1, B_Q_LORA), cst), pl.BlockSpec((1, B_KV_LORA), cst),
            pl.BlockSpec((B_Q_LORA, HW), cst), pl.BlockSpec((B_Q_LORA, HW), cst),
            pl.BlockSpec((B_KV_LORA, HW), cst), pl.BlockSpec((B_KV_LORA, HW), cst),
            row, row, row, row, tab, tab, row, row, pl.BlockSpec((1, HW), cst),
        ],
        out_specs=[pl.BlockSpec((tm, HW), lambda i: (i, 0))] * 3,
        out_shape=[out, out, out],
        compiler_params=_cparams(("parallel",)),
        name="mla_prep",
    )(proj, proj, proj, proj, lp['b_gqa'], lp['b_gkva'], lp['b_wq'], lp['b_wqsw'], lp['b_wk'], lp['b_wv'],
      lp['b_gq'], lp['b_gqsw'], lp['b_gk'], lp['b_gksw'], *tables, *lp['b_adds'])


def _flash_kernel(q_ref, k_ref, v_ref, o_ref, *, S, online):
    tq, tk = FLASH_TQ, FLASH_TK
    nk = S // tk
    heads = [slice(hh * PAD_HEAD, (hh + 1) * PAD_HEAD) for hh in range(2)]
    qs = [q_ref[:, sl] for sl in heads]
    a0 = jnp.zeros((tq, PAD_HEAD), F32)

    def probs(j, hh):
        r0 = pl.multiple_of(j * tk, tk)
        return jnp.exp2(_nt(qs[hh], k_ref[pl.ds(r0, tk), heads[hh]])).astype(BF16)

    def pv(j, hh, p):
        r0 = pl.multiple_of(j * tk, tk)
        return _mm(p, v_ref[pl.ds(r0, tk), heads[hh]])

    if online:
        def body(j, carry):
            r0 = pl.multiple_of(j * tk, tk)
            out = []
            for hh, sl in enumerate(heads):
                s = _nt(qs[hh], k_ref[pl.ds(r0, tk), sl])
                m, acc = carry[hh]
                mn = jnp.maximum(m, jnp.max(s, axis=-1, keepdims=True))
                out.append((mn, acc * jnp.exp2(m - mn) + _mm(jnp.exp2(s - mn).astype(BF16), v_ref[pl.ds(r0, tk), sl])))
            return tuple(out)

        init = (jnp.full((tq, 1), MASK_VALUE, F32), a0)
        res = lax.fori_loop(0, nk, body, (init, init))
        accs = [res[0][1], res[1][1]]
    else:
        def body(j, carry):
            out = []
            for hh in range(2):
                acc, p_prev = carry[hh]
                out.append((acc + pv(j - 1, hh, p_prev), probs(j, hh)))
            return tuple(out)

        res = lax.fori_loop(1, nk, body, ((a0, probs(0, 0)), (a0, probs(0, 1))), unroll=FLASH_UNROLL)
        accs = [res[hh][0] + pv(nk - 1, hh, res[hh][1]) for hh in range(2)]
    for hh in range(2):
        acc = accs[hh]
        o_ref[:, hh * B_V:(hh + 1) * B_V] = (acc[:, :B_V] / acc[:, SUM_LANE:SUM_LANE + 1]).astype(BF16)


def _flash(q, k, v, B, S, online):
    T = B * S
    tq = FLASH_TQ
    nq = S // tq
    return pl.pallas_call(
        functools.partial(_flash_kernel, S=S, online=online),
        grid=(B, B_HEADS // 2, nq),
        in_specs=[
            pl.BlockSpec((tq, 2 * PAD_HEAD), lambda b, h, i: (b * nq + i, h)),
            pl.BlockSpec((S, 2 * PAD_HEAD), lambda b, h, i: (b, h)),
            pl.BlockSpec((S, 2 * PAD_HEAD), lambda b, h, i: (b, h)),
        ],
        out_specs=pl.BlockSpec((tq, 2 * B_V), lambda b, h, i: (b * nq + i, h)),
        out_shape=jax.ShapeDtypeStruct((T, B_HEADS * B_V), BF16),
        compiler_params=_cparams(("parallel", "parallel", "parallel")),
        name="mla_flash_online" if online else "mla_flash",
    )(q, k, v)


def _hgrn_consts(L, reverse):
    levels = []
    m = 1
    while m < L:
        levels.append(m)
        m *= 2
    nl = len(levels)
    t = np.arange(L)
    wall = np.zeros((nl + 2, L, L), np.float32)
    right = np.zeros((nl, L, 1), np.float32)
    mask = np.zeros((nl + 1, L, L), np.float32)
    for li, m in enumerate(levels):
        blk = t // (2 * m)
        mid = blk * 2 * m + m
        isr = t >= mid
        for tt in range(L):
            if isr[tt]:
                wall[li, tt, mid[tt]:tt + 1] = 1.0
            else:
                wall[li, tt, tt + 1:mid[tt]] = 1.0
        right[li, :, 0] = isr
        mask[li] = (blk[:, None] == blk[None, :]) & isr[:, None] & (~isr[None, :])
    mask[nl] = np.eye(L)
    wall[nl] = np.tril(np.ones((L, L)))
    wall[nl + 1] = np.triu(np.ones((L, L)), 1)
    if reverse:
        wall = wall[:, ::-1, ::-1]
        right = right[:, ::-1]
        mask = mask[:, ::-1, ::-1]
    right = np.broadcast_to(right, (nl, L, C_DK))
    wall = wall.reshape((nl + 2) * L, L)
    return (jnp.asarray(np.ascontiguousarray(np.concatenate([wall, wall], axis=1)), BF16),
            jnp.asarray(np.ascontiguousarray(right)), jnp.asarray(np.ascontiguousarray(mask)), nl)


def _hgrn_direction(q_ref, z_ref, v_ref, lb, wall_ref, right_ref, mask_ref, st_ref, o_ref, *, L, nl, reverse):
    z = z_ref[...].astype(F32)
    sig = jax.nn.sigmoid(z)
    lf = jnp.log(jnp.maximum(lb + (1.0 - lb) * sig, TINY))
    key = (1.0 - lb) * (1.0 - sig)
    qx = q_ref[...].astype(F32)
    qh = qx * jax.nn.sigmoid(qx)
    hi = lf.astype(BF16)
    lo = (lf - hi.astype(F32)).astype(BF16)
    e_all = jnp.exp(_mm(wall_ref[...], jnp.concatenate([hi, lo], axis=0)))
    last = 0 if reverse else L - 1
    for h in range(C_HEADS):
        sl = slice(h * C_DK, (h + 1) * C_DK)
        qh_h, key_h, v_h = qh[:, sl], key[:, sl], v_ref[:, sl]
        scores = _nt(qh_h.astype(BF16), key_h.astype(BF16)) * mask_ref[nl]
        q_minus_k = qh_h - key_h
        for li in range(nl):
            x = ((key_h + right_ref[li] * q_minus_k) * e_all[li * L:(li + 1) * L, sl]).astype(BF16)
            scores = scores + _nt(x, x) * mask_ref[li]
        o = _mm(scores.astype(BF16), v_h)
        ep = e_all[nl * L:(nl + 1) * L, sl]
        er = e_all[(nl + 1) * L:(nl + 2) * L, sl]
        st = st_ref[h]
        o_ref[:, sl] = o + _nt((qh_h * ep).astype(BF16), st.astype(BF16))
        st_ref[h] = st * ep[last:last + 1] + _mm(v_h.T, (key_h * er).astype(BF16))


def _hgrn_kernel(qf_ref, zf_ref, vf_ref, qb_ref, zb_ref, vb_ref, lb_ref, wf_ref, rf_ref, mf_ref, wb_ref, rb_ref, mb_ref,
                 of_ref, ob_ref, st_ref, *, L, nl):
    @pl.when(pl.program_id(1) == 0)
    def _():
        st_ref[...] = jnp.zeros_like(st_ref)

    _hgrn_direction(qf_ref, zf_ref, vf_ref, lb_ref[0:1], wf_ref, rf_ref, mf_ref, st_ref.at[0], of_ref,
                    L=L, nl=nl, reverse=False)
    _hgrn_direction(qb_ref, zb_ref, vb_ref, lb_ref[1:2], wb_ref, rb_ref, mb_ref, st_ref.at[1], ob_ref,
                    L=L, nl=nl, reverse=True)


def _hgrn(proj, lb, consts_f, consts_b, B, S):
    T = B * S
    L = HGRN_CHUNK
    nc = S // L
    nl = consts_f[3]
    fwd = lambda b, c: b * nc + c
    bwd = lambda b, c: b * nc + nc - 1 - c
    pspec = lambda rows, off: pl.BlockSpec((L, 512), lambda b, c: (rows(b, c), off // 512))
    cst2 = lambda b, c: (0, 0)
    cst3 = lambda b, c: (0, 0, 0)
    cspecs = lambda cs: [pl.BlockSpec(cs[0].shape, cst2), pl.BlockSpec(cs[1].shape, cst3), pl.BlockSpec(cs[2].shape, cst3)]
    out = jax.ShapeDtypeStruct((T, 512), F32)
    return pl.pallas_call(
        functools.partial(_hgrn_kernel, L=L, nl=nl),
        grid=(B, nc),
        in_specs=[pspec(fwd, OFF_CQ), pspec(fwd, OFF_CFF), pspec(fwd, OFF_CI),
                  pspec(bwd, OFF_CQ), pspec(bwd, OFF_CFB), pspec(bwd, OFF_CI),
                  pl.BlockSpec((2, 512), cst2)] + cspecs(consts_f) + cspecs(consts_b),
        out_specs=[pl.BlockSpec((L, 512), lambda b, c: (fwd(b, c), 0)), pl.BlockSpec((L, 512), lambda b, c: (bwd(b, c), 0))],
        out_shape=[out, out],
        scratch_shapes=[pltpu.VMEM((2, C_HEADS, C_DV, C_DK), F32)],
        compiler_params=_cparams(("parallel", "arbitrary")),
        name="hgrn",
    )(proj, proj, proj, proj, proj, proj, lb, *consts_f[:3], *consts_b[:3])


def _fft_consts(S, B):
    N2 = FFT_N2
    N1 = S // N2
    nbat = math.gcd(B, max(1, FFT1_ROWS // N1))
    d = np.arange(D_GROUP_DIM)
    ang = 2.0 * np.pi * ((d[:, None] * d[None, :]) % D_GROUP_DIM) / D_GROUP_DIM
    wc = np.concatenate([np.cos(ang), -np.sin(ang)], axis=1) / math.sqrt(D_GROUP_DIM)
    a = np.arange(N1)
    ang1 = 2.0 * np.pi * ((a[:, None] * a[None, :]) % N1) / N1
    eye = np.eye(nbat)
    c1, s1 = np.kron(eye, np.cos(ang1)), np.kron(eye, np.sin(ang1))
    m1 = np.block([[c1, s1], [-s1, c1]])
    p1 = jnp.arange(N1, dtype=jnp.int32)[:, None, None]
    p2 = jnp.arange(N2, dtype=jnp.int32)[None, :, None]
    s2 = jnp.arange(N2, dtype=jnp.int32)[None, None, :]
    kk = (s2 * (p1 + N1 * p2)) % S
    th = kk.astype(F32) * (2.0 * math.pi / S)
    m2 = jnp.concatenate([jnp.cos(th), jnp.sin(th)], axis=2) * (1.0 / math.sqrt(S))
    return jnp.asarray(wc, BF16), jnp.asarray(m1, BF16), m2.astype(BF16), N1, nbat


def _fft1_kernel(u_ref, wc_ref, m1_ref, ar_ref, ai_ref, *, rows, ct):
    wc, m1 = wc_ref[...], m1_ref[...]
    for g in range(ct // LANES):
        sl = slice(g * LANES, (g + 1) * LANES)
        z = _mm(u_ref[:, sl], wc)
        zs = jnp.concatenate([z[:, :LANES], z[:, LANES:]], axis=0).astype(BF16)
        a = _mm(m1, zs)
        ar_ref[:, sl] = a[:rows].astype(BF16)
        ai_ref[:, sl] = a[rows:].astype(BF16)


def _fft2_kernel(ar_ref, ai_ref, m2_ref, o_ref):
    for p in range(FFT2_P):
        x = jnp.concatenate([ar_ref[p], ai_ref[p]], axis=0)
        o_ref[:, p * 512:(p + 1) * 512] = _mm(m2_ref[p], x).astype(BF16)


def _fourier(u, consts, B, S):
    wc, m1, m2, N1, nbat = consts
    N2 = FFT_N2
    ct = FFT1_COLS
    P = FFT2_P
    W = N2 * 512
    rows = nbat * N1
    u2 = u.reshape(B * N1, W)
    sds = jax.ShapeDtypeStruct((B * N1, W), BF16)
    ar, ai = pl.pallas_call(
        functools.partial(_fft1_kernel, rows=rows, ct=ct),
        grid=(B // nbat, W // ct),
        in_specs=[
            pl.BlockSpec((rows, ct), lambda b, j: (b, j)),
            pl.BlockSpec(wc.shape, lambda b, j: (0, 0)),
            pl.BlockSpec(m1.shape, lambda b, j: (0, 0)),
        ],
        out_specs=[pl.BlockSpec((rows, ct), lambda b, j: (b, j))] * 2,
        out_shape=[sds, sds],
        compiler_params=_cparams(("parallel", "parallel")),
        name="fft_stage1",
    )(u2, wc, m1)
    ar3 = ar.reshape(B * N1, N2, 512)
    ai3 = ai.reshape(B * N1, N2, 512)
    y = pl.pallas_call(
        _fft2_kernel,
        grid=(B, N1 // P),
        in_specs=[
            pl.BlockSpec((P, N2, 512), lambda b, p: (b * (N1 // P) + p, 0, 0)),
            pl.BlockSpec((P, N2, 512), lambda b, p: (b * (N1 // P) + p, 0, 0)),
            pl.BlockSpec((P, N2, 2 * N2), lambda b, p: (p, 0, 0)),
        ],
        out_specs=pl.BlockSpec((N2, P * 512), lambda b, p: (b, p)),
        out_shape=jax.ShapeDtypeStruct((B * N2, N1 * 512), BF16),
        compiler_params=_cparams(("parallel", "parallel")),
        name="fft_stage2",
    )(ar3, ai3, m2)
    return y.reshape(B * S, 512)


def _merge_kernel(x_ref, oa_ref, ob_ref, cf_ref, cb_ref, cg_ref, od_ref, g1_ref, gn_ref, wg_ref, wb_ref, wo_ref, g2_ref,
                  wr_ref, x1_ref, h2_ref, aff_ref):
    x = x_ref[...]
    h = _rms(x, g1_ref[...]).astype(BF16)
    tot = cf_ref[...] + cb_ref[...]
    gate = cg_ref[...].astype(F32)
    gate = gate * jax.nn.sigmoid(gate)
    o_c = jnp.concatenate([_rms(tot[:, hd * C_DV:(hd + 1) * C_DV], gn_ref[...]) for hd in range(C_HEADS)], axis=1)
    branches = (oa_ref[...], ob_ref[...], (o_c * gate).astype(BF16), od_ref[...])
    mixed = None
    for n in range(N_BRANCHES):
        term = jax.nn.sigmoid(_mm(h, wg_ref[n])) * _mm(branches[n], wb_ref[n])
        mixed = term if mixed is None else mixed + term
    x1 = x + _mm(mixed.astype(BF16), wo_ref[...])
    x1_ref[...] = x1
    h2 = _rms(x1, g2_ref[...]).astype(BF16)
    h2_ref[...] = h2
    logits = _nt(wr_ref[...], h2)
    mx = jnp.max(logits, axis=0, keepdims=True)
    ex = jnp.exp(logits - mx)
    aff_ref[...] = ex / jnp.sum(ex, axis=0, keepdims=True)


def _merge(x2, oa, ob, cf, cb, proj, od, g1, gn, wg, wb, wo, g2, wr_t):
    T = x2.shape[0]
    tm = TOKEN_TILE
    tok = lambda w: pl.BlockSpec((tm, w), lambda i: (i, 0))
    c2 = lambda i: (0, 0)
    c3 = lambda i: (0, 0, 0)
    return pl.pallas_call(
        _merge_kernel,
        grid=(T // tm,),
        in_specs=[
            tok(D_MODEL), tok(512), tok(512), tok(512), tok(512),
            pl.BlockSpec((tm, 512), lambda i: (i, OFF_CG // 512)), tok(512),
            pl.BlockSpec((1, D_MODEL), c2), pl.BlockSpec((1, C_DV), c2),
            pl.BlockSpec((N_BRANCHES, D_MODEL, D_MODEL), c3),
            pl.BlockSpec((N_BRANCHES, BRANCH_WIDTH, D_MODEL), c3),
            pl.BlockSpec((D_MODEL, D_MODEL), c2),
            pl.BlockSpec((1, D_MODEL), c2),
            pl.BlockSpec((N_EXPERTS, D_MODEL), c2),
        ],
        out_specs=[tok(D_MODEL), tok(D_MODEL), pl.BlockSpec((N_EXPERTS, tm), lambda i: (0, i))],
        out_shape=[jax.ShapeDtypeStruct((T, D_MODEL), F32), jax.ShapeDtypeStruct((T, D_MODEL), BF16),
                   jax.ShapeDtypeStruct((N_EXPERTS, T), F32)],
        compiler_params=_cparams(("parallel",)),
        name="merge",
    )(x2, oa, ob, cf, cb, proj, od, g1, gn, wg, wb, wo, g2, wr_t)


def _cap_rows(T):
    cap = CAPACITY_FACTOR * T // N_EXPERTS
    ntiles = T // MOE_TILE
    need = cap + BF16_SUBLANES * ntiles + max(FFN_ROWS, MOE_SUB)
    return cap, ntiles, -(-need // FFN_ROWS) * FFN_ROWS


def _route_kernel(aff_ref, tri_ref, rank_ref, cnt_ref, off_ref, *, T, cap, ntiles):
    tk = MOE_TILE
    bits = pltpu.bitcast(aff_ref[...], jnp.int32)
    capf = jnp.float32(cap)

    def count(pred):
        return jnp.sum(jnp.where(pred, 1.0, 0.0), axis=-1, keepdims=True)

    def vbody(i, v):
        cand = v | lax.shift_left(jnp.int32(1), 30 - i)
        return jnp.where(count(bits >= cand) >= capf, cand, v)

    v = lax.fori_loop(0, 31, vbody, jnp.zeros((N_EXPERTS, 1), jnp.int32))
    need = capf - count(bits > v)
    ties = bits == v
    idx = lax.broadcasted_iota(jnp.int32, (N_EXPERTS, T), 1)
    nbits = max(1, (T - 1).bit_length())

    def jbody(i, j0):
        cand = j0 | lax.shift_left(jnp.int32(1), nbits - 1 - i)
        return jnp.where(count(ties & (idx < cand)) < need, cand, j0)

    j0 = lax.fori_loop(0, nbits, jbody, jnp.zeros((N_EXPERTS, 1), jnp.int32))
    sel = (bits > v) | (ties & (idx <= j0))
    tri = tri_ref[...]
    cnt_ref[...] = jnp.zeros_like(cnt_ref)
    for i in range(ntiles):
        sl = slice(i * tk, (i + 1) * tk)
        s_i = sel[:, sl]
        sf = jnp.where(s_i, 1.0, 0.0)
        r = _mm(sf.astype(BF16), tri)
        rank_ref[:, sl] = jnp.where(s_i, r, -1.0)
        cnt_ref[:, i:i + 1] = jnp.sum(sf, axis=-1, keepdims=True)
    cnt = cnt_ref[...]
    units = jnp.ceil(cnt * (1.0 / BF16_SUBLANES))
    off_ref[...] = (_mm(units.astype(BF16), tri[:LANES, :LANES]) * BF16_SUBLANES).astype(jnp.int32)


def _route(aff_t, tri, T):
    cap, ntiles, _ = _cap_rows(T)
    assert ntiles <= LANES and MOE_TILE // BF16_SUBLANES <= 256
    full = lambda s: pl.BlockSpec(s, lambda: tuple(0 for _ in s))
    rank, cnt, off = pl.pallas_call(
        functools.partial(_route_kernel, T=T, cap=cap, ntiles=ntiles),
        in_specs=[full((N_EXPERTS, T)), full(tri.shape)],
        out_specs=[full((N_EXPERTS, T)), full((N_EXPERTS, LANES)), full((N_EXPERTS, LANES))],
        out_shape=[jax.ShapeDtypeStruct((N_EXPERTS, T), F32), jax.ShapeDtypeStruct((N_EXPERTS, LANES), F32),
                   jax.ShapeDtypeStruct((N_EXPERTS, LANES), jnp.int32)],
        compiler_params=_cparams(()),
        name="moe_route",
    )(aff_t, tri)
    return rank, cnt.astype(jnp.int32), off


def _gather_kernel(cnt_ref, off_ref, rank_ref, h_ref, xe_ref, stage_ref, extra_ref, zero_ref, sem, xsem, *, cap):
    i, n = pl.program_id(0), pl.num_programs(0)
    R = MOE_SUB
    slot = i % 2
    h = h_ref[...]
    jrow = lax.broadcasted_iota(jnp.int32, (R, MOE_TILE), 0).astype(F32)
    zrows = zero_ref.shape[0]
    nzero = (xe_ref.shape[1] - cap) // zrows

    @pl.when(i == 0)
    def _():
        zero_ref[...] = jnp.zeros_like(zero_ref)

        def zero_copy(e, k):
            return pltpu.make_async_copy(zero_ref, xe_ref.at[e, pl.ds(cap + k * zrows, zrows)], sem.at[0, e])

        for k in range(nzero):
            for e in range(N_EXPERTS):
                zero_copy(e, k).start()
            for e in range(N_EXPERTS):
                zero_copy(e, k).wait()

    def rows(e, r):
        hit = rank_ref[e:e + 1, :] == jrow + (r * R).astype(F32)
        return _mm(jnp.where(hit, 1.0, 0.0).astype(BF16), h).astype(BF16)

    def out_copy(step, sl, e):
        r0 = pl.multiple_of(off_ref[e, step], BF16_SUBLANES)
        return pltpu.make_async_copy(stage_ref.at[sl, e], xe_ref.at[e, pl.ds(r0, R)], sem.at[sl, e])

    for e in range(N_EXPERTS):
        stage_ref[slot, e] = rows(e, jnp.int32(0))

    @pl.when(i > 0)
    def _():
        for e in range(N_EXPERTS):
            out_copy(i - 1, 1 - slot, e).wait()

    for e in range(N_EXPERTS):
        out_copy(i, slot, e).start()

    for e in range(N_EXPERTS):
        def body(r, carry, e=e):
            extra_ref[...] = rows(e, r)
            r0 = pl.multiple_of(off_ref[e, i] + r * R, BF16_SUBLANES)
            cp = pltpu.make_async_copy(extra_ref, xe_ref.at[e, pl.ds(r0, R)], xsem.at[0])
            cp.start()
            cp.wait()
            return carry

        lax.fori_loop(1, (cnt_ref[e, i] + R - 1) // R, body, 0)

    @pl.when(i == n - 1)
    def _():
        for e in range(N_EXPERTS):
            out_copy(i, slot, e).wait()


def _gather(cnt, off, rank, h2, T):
    cap, ntiles, rows = _cap_rows(T)
    tk = MOE_TILE
    R = MOE_SUB
    return pl.pallas_call(
        functools.partial(_gather_kernel, cap=cap),
        grid_spec=pltpu.PrefetchScalarGridSpec(
            num_scalar_prefetch=2,
            grid=(ntiles,),
            in_specs=[
                pl.BlockSpec((N_EXPERTS, tk), lambda i, c, o: (0, i)),
                pl.BlockSpec((tk, D_MODEL), lambda i, c, o: (i, 0)),
            ],
            out_specs=pl.BlockSpec(memory_space=pl.ANY),
            scratch_shapes=[
                pltpu.VMEM((2, N_EXPERTS, R, D_MODEL), BF16),
                pltpu.VMEM((R, D_MODEL), BF16),
                pltpu.VMEM((math.gcd(rows - cap, FFN_ROWS), D_MODEL), BF16),
                pltpu.SemaphoreType.DMA((2, N_EXPERTS)),
                pltpu.SemaphoreType.DMA((1,)),
            ],
        ),
        out_shape=jax.ShapeDtypeStruct((N_EXPERTS, rows, D_MODEL), BF16),
        compiler_params=_cparams(("arbitrary",)),
        name="moe_gather",
    )(cnt, off, rank, h2)


def _ffn_kernel(used_ref, x_ref, wg_ref, wu_ref, wd_ref, y_ref, wg_s, wu_s, wd_s):
    live = pl.program_id(1) * FFN_ROWS < used_ref[pl.program_id(0)]

    @pl.when(pl.program_id(1) == 0)
    def _():
        wg_s[...] = wg_ref[0].astype(BF16)
        wu_s[...] = wu_ref[0].astype(BF16)
        wd_s[...] = wd_ref[0].astype(BF16)

    @pl.when(live)
    def _():
        x = x_ref[0]
        a = _mm(x, wg_s[...])
        u = _mm(x, wu_s[...])
        hmid = (a * jax.nn.sigmoid(a) * u).astype(BF16)
        y_ref[0] = _mm(hmid, wd_s[...]).astype(BF16)

    @pl.when(jnp.logical_not(live))
    def _():
        y_ref[...] = jnp.zeros_like(y_ref)


def _ffn(used, xe, wg, wu, wd, layer):
    E, rows, _ = xe.shape
    wspec = lambda s: pl.BlockSpec((None, 1) + s, lambda e, j, u: (layer, e, 0, 0))
    return pl.pallas_call(
        _ffn_kernel,
        grid_spec=pltpu.PrefetchScalarGridSpec(
            num_scalar_prefetch=1,
            grid=(E, rows // FFN_ROWS),
            in_specs=[
                pl.BlockSpec((1, FFN_ROWS, D_MODEL), lambda e, j, u: (e, j, 0)),
                wspec((D_MODEL, EXPERT_FF)), wspec((D_MODEL, EXPERT_FF)), wspec((EXPERT_FF, D_MODEL)),
            ],
            out_specs=pl.BlockSpec((1, FFN_ROWS, D_MODEL), lambda e, j, u: (e, j, 0)),
            scratch_shapes=[pltpu.VMEM((D_MODEL, EXPERT_FF), BF16), pltpu.VMEM((D_MODEL, EXPERT_FF), BF16),
                            pltpu.VMEM((EXPERT_FF, D_MODEL), BF16)],
        ),
        out_shape=jax.ShapeDtypeStruct(xe.shape, BF16),
        compiler_params=_cparams(("parallel", "arbitrary")),
        name="moe_ffn",
    )(used, xe, wg, wu, wd)


def _scatter_kernel(cnt_ref, off_ref, x_ref, rank_ref, aff_ref, ye_ref, o_ref, seg_ref, oh_ref, extra_ref, sem, xsem):
    i, n = pl.program_id(0), pl.num_programs(0)
    R = MOE_SUB
    tk = MOE_TILE
    slot = i % 2

    def seg_copy(step, sl, e):
        r0 = pl.multiple_of(off_ref[e, step], BF16_SUBLANES)
        return pltpu.make_async_copy(ye_ref.at[e, pl.ds(r0, R)], seg_ref.at[sl, pl.ds(e * R, R)], sem.at[sl, e])

    @pl.when(i == 0)
    def _():
        for e in range(N_EXPERTS):
            seg_copy(i, slot, e).start()

    @pl.when(i + 1 < n)
    def _():
        for e in range(N_EXPERTS):
            seg_copy(i + 1, 1 - slot, e).start()

    jlane = lax.broadcasted_iota(jnp.int32, (tk, R), 1).astype(F32)
    for e in range(N_EXPERTS):
        oh_ref[:, e * R:(e + 1) * R] = jnp.where(rank_ref[:, e:e + 1] == jlane, aff_ref[:, e:e + 1], 0.0).astype(BF16)
    for e in range(N_EXPERTS):
        seg_copy(i, slot, e).wait()
    o_ref[...] = x_ref[...] + _mm(oh_ref[...], seg_ref[slot])

    for e in range(N_EXPERTS):
        def body(r, carry, e=e):
            r0 = pl.multiple_of(off_ref[e, i] + r * R, BF16_SUBLANES)
            cp = pltpu.make_async_copy(ye_ref.at[e, pl.ds(r0, R)], extra_ref, xsem.at[0])
            cp.start()
            hit = rank_ref[:, e:e + 1] == jlane + (r * R).astype(F32)
            w = jnp.where(hit, aff_ref[:, e:e + 1], 0.0).astype(BF16)
            cp.wait()
            o_ref[...] += _mm(w, extra_ref[...])
            return carry

        lax.fori_loop(1, (cnt_ref[e, i] + R - 1) // R, body, 0)


def _scatter(cnt, off, x1, rank_t, aff_tok, ye, T):
    _, ntiles, rows = _cap_rows(T)
    tk = MOE_TILE
    R = MOE_SUB
    return pl.pallas_call(
        _scatter_kernel,
        grid_spec=pltpu.PrefetchScalarGridSpec(
            num_scalar_prefetch=2,
            grid=(ntiles,),
            in_specs=[
                pl.BlockSpec((tk, D_MODEL), lambda i, c, o: (i, 0)),
                pl.BlockSpec((tk, N_EXPERTS), lambda i, c, o: (i, 0)),
                pl.BlockSpec((tk, N_EXPERTS), lambda i, c, o: (i, 0)),
                pl.BlockSpec(memory_space=pl.ANY),
            ],
            out_specs=pl.BlockSpec((tk, D_MODEL), lambda i, c, o: (i, 0)),
            scratch_shapes=[
                pltpu.VMEM((2, N_EXPERTS * R, D_MODEL), BF16),
                pltpu.VMEM((tk, N_EXPERTS * R), BF16),
                pltpu.VMEM((R, D_MODEL), BF16),
                pltpu.SemaphoreType.DMA((2, N_EXPERTS)),
                pltpu.SemaphoreType.DMA((1,)),
            ],
        ),
        out_shape=jax.ShapeDtypeStruct((T, D_MODEL), F32),
        compiler_params=_cparams(("arbitrary",)),
        name="moe_scatter",
    )(cnt, off, x1, rank_t, aff_tok, ye)


def _expert_choice(x1, h2, aff_t, tri, wg, wu, wd, layer):
    T = x1.shape[0]
    rank, cnt, off = _route(aff_t, tri, T)
    ntiles = T // MOE_TILE
    used = off[:, ntiles - 1] + -(-cnt[:, ntiles - 1] // BF16_SUBLANES) * BF16_SUBLANES
    xe = _gather(cnt, off, rank, h2, T)
    ye = _ffn(used, xe, wg, wu, wd, layer)
    return _scatter(cnt, off, x1, rank.T, aff_t.T, ye, T)


def _pad_heads(w, n_heads, width, pad_to):
    k = w.shape[0]
    w = w.reshape(k, n_heads, width)
    return jnp.pad(w, ((0, 0), (0, 0), (0, pad_to - width))).reshape(k, n_heads * pad_to)


def _pack_layer(l, p):
    w_in = p['w_in'][l]
    edges = np.cumsum([0, 512, 128, 128, B_Q_LORA, B_KV_LORA + B_ROPE, 512, 512, 512, 512, 512, 512])
    a_q, a_k, a_v, b_cq, b_ckv, c_q, c_ff, c_fb, c_i, c_g, d_u = [w_in[:, edges[j]:edges[j + 1]] for j in range(11)]
    zeros = lambda n: jnp.zeros((D_MODEL, n), w_in.dtype)
    b_kpe = jnp.concatenate([zeros(B_NOPE), b_ckv[:, B_KV_LORA:], zeros(PAD_HEAD - B_QK)], axis=1)
    w_main = jnp.concatenate([a_q, c_q, c_ff, c_fb, c_i, c_g, a_k, a_v, b_cq, b_ckv[:, :B_KV_LORA], b_kpe,
                              _swap_rope_halves(b_kpe)], axis=1).astype(BF16)
    kv_up = p['b_kv_up'][l].reshape(B_KV_LORA, B_HEADS, B_NOPE + B_V)
    pad_g = lambda g: jnp.pad(g, (0, PAD_HEAD - B_QK)).reshape(1, PAD_HEAD)
    b_wq = _pad_heads(p['b_q_up'][l], B_HEADS, B_QK, PAD_HEAD)
    b_wqsw = _swap_rope_halves(b_wq.reshape(B_Q_LORA, B_HEADS, PAD_HEAD)).reshape(B_Q_LORA, B_HEADS * PAD_HEAD)
    b_gq, b_gk = pad_g(p['b_q_norm_g'][l]), pad_g(p['b_k_norm_g'][l])
    sink = p['a_sink'][l].astype(F32).reshape(A_KV_HEADS, A_HEADS // A_KV_HEADS)
    b_bound = math.sqrt(B_QK) * jnp.max(jnp.abs(p['b_q_norm_g'][l])) * jnp.max(jnp.abs(p['b_k_norm_g'][l]))
    lane = jnp.arange(PAD_HEAD)
    b_qadd = jnp.where(lane == SCORE_LANE, 1.0, 0.0).astype(F32).reshape(1, PAD_HEAD)
    b_kadd = jnp.where(lane == SCORE_LANE, -LOG2E * b_bound, 0.0).astype(F32).reshape(1, PAD_HEAD)
    b_vadd = jnp.tile(jnp.where(lane == SUM_LANE, 1.0, 0.0).astype(F32), B_HEADS).reshape(1, B_HEADS * PAD_HEAD)
    a_bound, a_bias, a_sink_term = _win_fast_tables(p['a_q_norm_g'][l], p['a_k_norm_g'][l], p['a_sink'][l])
    return dict(
        a_bound=a_bound, a_bias=a_bias, a_sink_term=a_sink_term,
        a_gq_t=(jnp.tile(p['a_q_norm_g'][l], A_HEADS) * (LOG2E * A_HEAD_DIM ** -0.5)).reshape(1, A_HEADS * A_HEAD_DIM),
        a_gk_t=jnp.tile(p['a_k_norm_g'][l], A_KV_HEADS).reshape(1, A_KV_HEADS * A_HEAD_DIM),
        b_bound=b_bound, b_adds=(b_qadd, b_kadd, b_vadd),
        g_mix=p['norm_mix_g'][l].reshape(1, D_MODEL), w_main=w_main, w_u=d_u.astype(BF16),
        a_gq=p['a_q_norm_g'][l].reshape(1, A_HEAD_DIM), a_gk=p['a_k_norm_g'][l].reshape(1, A_HEAD_DIM),
        a_sink=jnp.repeat(sink, WINDOW, axis=1)[:, :, None],
        b_gqa=p['b_q_a_norm_g'][l].reshape(1, B_Q_LORA), b_gkva=p['b_kv_a_norm_g'][l].reshape(1, B_KV_LORA),
        b_wq=b_wq.astype(BF16), b_wqsw=b_wqsw.astype(BF16),
        b_wk=_pad_heads(kv_up[:, :, :B_NOPE].reshape(B_KV_LORA, -1), B_HEADS, B_NOPE, PAD_HEAD).astype(BF16),
        b_wv=_pad_heads(kv_up[:, :, B_NOPE:].reshape(B_KV_LORA, -1), B_HEADS, B_V, PAD_HEAD).astype(BF16),
        b_gq=b_gq, b_gk=b_gk, b_gqsw=_swap_rope_halves(b_gq), b_gksw=_swap_rope_halves(b_gk),
        c_gn=p['c_out_norm_g'][l].reshape(1, C_DV),
        w_gate=p['w_gate'][l].astype(BF16), w_branch=p['w_branch'][l].astype(BF16), w_out=p['w_out'][l].astype(BF16),
        g_ffn=p['norm_ffn_g'][l].reshape(1, D_MODEL), w_router_t=p['w_router'][l].T.astype(BF16),
        layer=l, w_e_gate=p['w_e_gate'], w_e_up=p['w_e_up'], w_e_down=p['w_e_down'],
    )


def _layer(x2, B, S, lp, lb, cst):
    proj, u = _in_proj(x2, lp['g_mix'], lp['w_main'], lp['w_u'])
    o_a = lax.cond(lp['a_bound'] <= FAST_SOFTMAX_MAX_BOUND,
                   lambda pr: _win_attn_fast(pr, lp['a_gq_t'], lp['a_gk_t'], cst['win'], lp['a_bias'], lp['a_sink_term'], B, S),
                   lambda pr: _win_attn(pr, lp['a_gq'], lp['a_gk'], lp['a_sink'], B, S), proj)
    q, k, v = _mla_prep(proj, lp, cst['rope'][S], S)
    o_b = lax.cond(lp['b_bound'] <= FAST_SOFTMAX_MAX_BOUND,
                   lambda q, k, v: _flash(q, k, v, B, S, False),
                   lambda q, k, v: _flash(q, k, v, B, S, True), q, k, v)
    c_f, c_b = _hgrn(proj, lb, cst['hgrn_f'], cst['hgrn_b'], B, S)
    o_d = _fourier(u, cst['fft'][S], B, S)
    x1, h2, aff_t = _merge(x2, o_a, o_b, c_f, c_b, proj, o_d, lp['g_mix'], lp['c_gn'], lp['w_gate'], lp['w_branch'],
                           lp['w_out'], lp['g_ffn'], lp['w_router_t'])
    return _expert_choice(x1, h2, aff_t, cst['tri'], lp['w_e_gate'], lp['w_e_up'], lp['w_e_down'], lp['layer'])


def kernel(x_prompt, x_sample, norm_mix_g, w_in, a_q_norm_g, a_k_norm_g, a_sink, b_q_a_norm_g, b_q_up,
           b_kv_a_norm_g, b_kv_up, b_q_norm_g, b_k_norm_g, c_lb_logits, c_out_norm_g, w_branch, w_gate, w_out,
           norm_ffn_g, w_router, w_e_gate, w_e_up, w_e_down):
    p = dict(norm_mix_g=norm_mix_g, w_in=w_in, a_q_norm_g=a_q_norm_g, a_k_norm_g=a_k_norm_g, a_sink=a_sink,
             b_q_a_norm_g=b_q_a_norm_g, b_q_up=b_q_up, b_kv_a_norm_g=b_kv_a_norm_g, b_kv_up=b_kv_up,
             b_q_norm_g=b_q_norm_g, b_k_norm_g=b_k_norm_g, c_out_norm_g=c_out_norm_g, w_branch=w_branch,
             w_gate=w_gate, w_out=w_out, norm_ffn_g=norm_ffn_g, w_router=w_router, w_e_gate=w_e_gate,
             w_e_up=w_e_up, w_e_down=w_e_down)
    lb_w = jax.nn.softmax(c_lb_logits.astype(F32), axis=0)
    lb_all = jnp.cumsum(lb_w, axis=0) - lb_w[0:1]
    groups = [x_prompt, x_sample]
    seqs = sorted({g.shape[1] for g in groups})
    tk = MOE_TILE
    cst = dict(
        rope={S: _rope_tables(S) for S in seqs},
        fft={g.shape[1]: _fft_consts(g.shape[1], g.shape[0]) for g in groups},
        hgrn_f=_hgrn_consts(HGRN_CHUNK, False), hgrn_b=_hgrn_consts(HGRN_CHUNK, True),
        win=_win_fast_consts(),
        tri=jnp.asarray(np.triu(np.ones((tk, tk), np.float32), 1), BF16),
    )
    layers = [_pack_layer(l, p) for l in range(DEPTH)]
    outs = []
    for xg in groups:
        B, S, _ = xg.shape
        y = xg.reshape(B * S, D_MODEL)
        for l in range(DEPTH):
            y = _layer(y, B, S, layers[l], lb_all[l], cst)
        outs.append(y.reshape(B, S, D_MODEL))
    return tuple(outs)
```

```python
import functools
import math

import numpy as np
import jax
import jax.numpy as jnp
from jax import lax
from jax.experimental import pallas as pl
from jax.experimental.pallas import tpu as pltpu

F32 = jnp.float32
BF16 = jnp.bfloat16

D_MODEL = 1024
DEPTH = 2
A_HEADS, A_KV_HEADS, A_HEAD_DIM, WINDOW = 8, 2, 64, 128
B_HEADS, B_Q_LORA, B_KV_LORA, B_NOPE, B_ROPE, B_V = 8, 256, 128, 64, 32, 64
B_QK = B_NOPE + B_ROPE
ROPE_THETA = 10000.0
C_HEADS, C_DK, C_DV = 4, 128, 128
D_GROUPS, D_GROUP_DIM = 4, 128
N_BRANCHES, BRANCH_WIDTH = 4, 512
N_EXPERTS, EXPERT_FF, CAPACITY_FACTOR = 16, 1024, 2
EPS = 1e-6
MASK_VALUE = -1e30
TINY = 1e-30
LOG2E = 1.4426950408889634

SCORE_LANE = B_QK
SUM_LANE = B_V
FAST_SOFTMAX_MAX_BOUND = 30.0

LANES = 128
BF16_SUBLANES = 16
V7X_VMEM_LIMIT = 56 * 1024 * 1024

TOKEN_TILE = 512
PAD_HEAD = 128
HGRN_CHUNK = 128
FFT_N2 = 128
FFT1_COLS = 4096
FFT1_ROWS = 64
FFT2_P = 4
FLASH_TQ, FLASH_TK, FLASH_UNROLL = 512, 512, 5
MOE_TILE = 512
MOE_SUB = 128
FFN_ROWS = 512

OFF_AQ, OFF_CQ, OFF_CFF, OFF_CFB, OFF_CI, OFF_CG = 0, 512, 1024, 1536, 2048, 2560
OFF_AK, OFF_AV, OFF_BCQ, OFF_BLAT, OFF_BKPE, OFF_BKPESW = 3072, 3200, 3328, 3584, 3712, 3840
PROJ_W = 3968


def _cparams(sem, vmem=V7X_VMEM_LIMIT):
    return pltpu.CompilerParams(dimension_semantics=sem, vmem_limit_bytes=vmem)


def _nt(a, b):
    return lax.dot_general(a, b, (((1,), (1,)), ((), ())), preferred_element_type=F32)


def _mm(a, b):
    return jnp.dot(a, b, preferred_element_type=F32)


def _rms(x, g):
    return x * lax.rsqrt(jnp.mean(x * x, axis=-1, keepdims=True) + EPS) * g


def _in_proj_kernel(x_ref, g_ref, w_ref, wu_ref, p_ref, u_ref):
    h = _rms(x_ref[...], g_ref[...]).astype(BF16)
    for c0 in range(0, PROJ_W, 512):
        c1 = min(c0 + 512, PROJ_W)
        p_ref[:, c0:c1] = _mm(h, w_ref[:, c0:c1]).astype(BF16)
    u_ref[...] = _mm(h, wu_ref[...]).astype(BF16)


def _in_proj(x2, g, w_main, w_u):
    T = x2.shape[0]
    tm = TOKEN_TILE
    return pl.pallas_call(
        _in_proj_kernel,
        grid=(T // tm,),
        in_specs=[
            pl.BlockSpec((tm, D_MODEL), lambda i: (i, 0)),
            pl.BlockSpec((1, D_MODEL), lambda i: (0, 0)),
            pl.BlockSpec((D_MODEL, PROJ_W), lambda i: (0, 0)),
            pl.BlockSpec((D_MODEL, 512), lambda i: (0, 0)),
        ],
        out_specs=[
            pl.BlockSpec((tm, PROJ_W), lambda i: (i, 0)),
            pl.BlockSpec((tm, 512), lambda i: (i, 0)),
        ],
        out_shape=[jax.ShapeDtypeStruct((T, PROJ_W), BF16), jax.ShapeDtypeStruct((T, 512), BF16)],
        compiler_params=_cparams(("parallel",)),
        name="in_proj",
    )(x2, g, w_main, w_u)


def _win_attn_kernel(q_ref, kp_ref, kc_ref, kn_ref, vp_ref, vc_ref, vn_ref, gq_ref, gk_ref, sink_ref, o_ref, *, nb):
    n = pl.program_id(1)
    W = WINDOW
    G = A_HEADS // A_KV_HEADS
    q = q_ref[...].astype(F32)
    kcat = jnp.concatenate([kp_ref[...], kc_ref[...], kn_ref[...]], axis=0).astype(F32)
    vcat = jnp.concatenate([vp_ref[...], vc_ref[...], vn_ref[...]], axis=0)
    gq = gq_ref[...]
    gk = gk_ref[...]
    row = lax.broadcasted_iota(jnp.int32, (G * W, 3 * W), 0)
    col = lax.broadcasted_iota(jnp.int32, (G * W, 3 * W), 1)
    qi = row % W
    dist = jnp.abs(col - W - qi)
    valid = dist <= W
    valid = valid & ((col >= W) | (n > 0)) & ((col < 2 * W) | (n < nb - 1))
    distf = dist.astype(F32)
    hrow = lax.broadcasted_iota(jnp.int32, (G * W, 1), 0) // W
    for g in range(A_KV_HEADS):
        qs = []
        for j in range(G):
            h = g * G + j
            qs.append(_rms(q[:, h * A_HEAD_DIM:(h + 1) * A_HEAD_DIM], gq))
        qg = (jnp.concatenate(qs, axis=0) * (A_HEAD_DIM ** -0.5)).astype(BF16)
        kg = _rms(kcat[:, g * A_HEAD_DIM:(g + 1) * A_HEAD_DIM], gk).astype(BF16)
        vg = vcat[:, g * A_HEAD_DIM:(g + 1) * A_HEAD_DIM]
        slope = jnp.exp2(-(hrow + (g * G + 1)).astype(F32))
        s = _nt(qg, kg) - slope * distf
        s = jnp.where(valid, s, MASK_VALUE)
        sk = sink_ref[g]
        m = jnp.maximum(jnp.max(s, axis=-1, keepdims=True), sk)
        p = jnp.exp(s - m)
        den = jnp.sum(p, axis=-1, keepdims=True) + jnp.exp(sk - m)
        o = _mm(p.astype(BF16), vg) / den
        for j in range(G):
            h = g * G + j
            o_ref[:, h * A_HEAD_DIM:(h + 1) * A_HEAD_DIM] = o[j * W:(j + 1) * W].astype(BF16)


def _win_fast_consts():
    d = A_HEAD_DIM
    lane = np.arange(A_HEADS * d)
    return jnp.asarray((lane[:, None] // d == lane[None, :] // d).astype(np.float32) / d, BF16)


def _win_fast_tables(gq, gk, sink):
    W = WINDOW
    bound = math.sqrt(A_HEAD_DIM) * jnp.max(jnp.abs(gq)) * jnp.max(jnp.abs(gk))
    sink = sink.astype(F32)
    m = jnp.maximum(bound, sink)
    slopes = 2.0 ** (-8.0 * jnp.arange(1, A_HEADS + 1, dtype=F32) / A_HEADS)
    r = jnp.arange(W)[:, None]
    c = jnp.arange(3 * W)[None, :]
    dist = jnp.abs(c - W - r)
    inside = dist <= W
    variants = []
    for first, last in ((False, False), (True, False), (False, True), (True, True)):
        valid = inside & ((c >= W) | (not first)) & ((c < 2 * W) | (not last))
        b = LOG2E * (-slopes[:, None, None] * dist[None].astype(F32) - m[:, None, None])
        variants.append(jnp.where(valid[None], b, MASK_VALUE))
    sink_term = jnp.exp2(LOG2E * (sink - m))
    return bound, jnp.stack(variants), jnp.broadcast_to(sink_term[:, None, None], (A_HEADS, 1, LANES))


def _win_fast_kernel(q_ref, kp_ref, kc_ref, kn_ref, vp_ref, vc_ref, vn_ref, gq_ref, gk_ref, bd_ref, bias_ref, st_ref,
                     o_ref):
    d = A_HEAD_DIM
    q = q_ref[...].astype(F32)
    bd = bd_ref[...]
    qf = q * lax.rsqrt(_mm((q * q).astype(BF16), bd) + EPS) * gq_ref[...]
    kcat = jnp.concatenate([kp_ref[...], kc_ref[...], kn_ref[...]], axis=0).astype(F32)
    kss = _mm((kcat * kcat).astype(BF16), bd[:LANES, :LANES])
    kn = kcat * lax.rsqrt(kss + EPS) * gk_ref[...]
    vcat = jnp.concatenate([vp_ref[...], vc_ref[...], vn_ref[...]], axis=0).astype(F32)
    ones = jnp.ones((3 * WINDOW, LANES), BF16)
    lane = lax.broadcasted_iota(jnp.int32, (1, LANES), 1)
    half = [jnp.where(lane < d, 1.0, 0.0), jnp.where(lane >= d, 1.0, 0.0)]
    kn_sw, v_sw = pltpu.roll(kn, d, 1), pltpu.roll(vcat, d, 1)
    G = A_HEADS // A_KV_HEADS
    for g in range(A_KV_HEADS):
        kk = jnp.where((lane < d) == (g == 0), kn, kn_sw).astype(BF16)
        rhs = [jnp.concatenate([((vcat if par == g else v_sw) * half[par]).astype(BF16), ones], axis=1)
               for par in range(2)]
        for jj in range(G // 2):
            pair = g * (G // 2) + jj
            qp = qf[:, pair * LANES:(pair + 1) * LANES]
            acc = None
            for par in range(2):
                h = 2 * pair + par
                s = _nt((qp * half[par]).astype(BF16), kk) + bias_ref[0, h]
                res = _mm(jnp.exp2(s).astype(BF16), rhs[par])
                o = res[:, :LANES] / (res[:, LANES:] + st_ref[h])
                acc = o if acc is None else acc + o
            o_ref[:, pair * LANES:(pair + 1) * LANES] = acc.astype(BF16)


def _win_attn_fast(proj, gq_t, gk_t, consts, bias, sink_term, B, S):
    T = B * S
    W = WINDOW
    nb = S // W
    kcol, vcol = OFF_AK // 128, OFF_AV // 128
    bd = consts
    cur = lambda b, n: b * nb + n
    prev = lambda b, n: b * nb + jnp.maximum(n - 1, 0)
    nxt = lambda b, n: b * nb + jnp.minimum(n + 1, nb - 1)
    spec = lambda rowf, c: pl.BlockSpec((W, 128), lambda b, n: (rowf(b, n), c))
    variant = lambda b, n: ((n == 0).astype(jnp.int32) + 2 * (n == nb - 1).astype(jnp.int32), 0, 0, 0)
    c2 = lambda b, n: (0, 0)
    c3 = lambda b, n: (0, 0, 0)
    return pl.pallas_call(
        _win_fast_kernel,
        grid=(B, nb),
        in_specs=[
            pl.BlockSpec((W, 512), lambda b, n: (cur(b, n), OFF_AQ // 512)),
            spec(prev, kcol), spec(cur, kcol), spec(nxt, kcol),
            spec(prev, vcol), spec(cur, vcol), spec(nxt, vcol),
            pl.BlockSpec((1, 512), c2), pl.BlockSpec((1, LANES), c2),
            pl.BlockSpec(bd.shape, c2),
            pl.BlockSpec((1, A_HEADS, W, 3 * W), variant),
            pl.BlockSpec((A_HEADS, 1, LANES), c3),
        ],
        out_specs=pl.BlockSpec((W, 512), lambda b, n: (cur(b, n), 0)),
        out_shape=jax.ShapeDtypeStruct((T, 512), BF16),
        compiler_params=_cparams(("parallel", "parallel")),
        name="win_attn_fast",
    )(proj, proj, proj, proj, proj, proj, proj, gq_t, gk_t, bd, bias, sink_term)


def _win_attn(proj, gq, gk, sink_cols, B, S):
    T = B * S
    W = WINDOW
    nb = S // W
    kcol, vcol = OFF_AK // 128, OFF_AV // 128

    def cur(b, n):
        return b * nb + n

    def prev(b, n):
        return b * nb + jnp.maximum(n - 1, 0)

    def nxt(b, n):
        return b * nb + jnp.minimum(n + 1, nb - 1)

    def spec(rowf, c):
        return pl.BlockSpec((W, 128), lambda b, n: (rowf(b, n), c))

    return pl.pallas_call(
        functools.partial(_win_attn_kernel, nb=nb),
        grid=(B, nb),
        in_specs=[
            pl.BlockSpec((W, 512), lambda b, n: (cur(b, n), OFF_AQ // 512)),
            spec(prev, kcol), spec(cur, kcol), spec(nxt, kcol),
            spec(prev, vcol), spec(cur, vcol), spec(nxt, vcol),
            pl.BlockSpec((1, A_HEAD_DIM), lambda b, n: (0, 0)),
            pl.BlockSpec((1, A_HEAD_DIM), lambda b, n: (0, 0)),
            pl.BlockSpec((A_KV_HEADS, 4 * W, 1), lambda b, n: (0, 0, 0)),
        ],
        out_specs=pl.BlockSpec((W, 512), lambda b, n: (cur(b, n), 0)),
        out_shape=jax.ShapeDtypeStruct((T, 512), BF16),
        compiler_params=_cparams(("parallel", "parallel")),
        name="win_attn",
    )(proj, proj, proj, proj, proj, proj, proj, gq, gk, sink_cols)


def _rope_tables(S):
    half = B_ROPE // 2
    inv = 1.0 / (ROPE_THETA ** (np.arange(half, dtype=np.float32) / half))
    ang = np.arange(S, dtype=np.float32)[:, None] * inv[None, :]
    cos, sin = np.cos(ang), np.sin(ang)
    cosf = np.ones((S, PAD_HEAD), np.float32)
    sinf = np.zeros((S, PAD_HEAD), np.float32)
    cosf[:, B_NOPE:B_NOPE + half] = cos
    cosf[:, B_NOPE + half:B_NOPE + 2 * half] = cos
    sinf[:, B_NOPE:B_NOPE + half] = -sin
    sinf[:, B_NOPE + half:B_NOPE + 2 * half] = sin
    return jnp.asarray(cosf), jnp.asarray(sinf)


def _swap_rope_halves(w):
    half = B_ROPE // 2
    z = jnp.zeros_like(w)
    z = z.at[..., B_NOPE:B_NOPE + half].set(w[..., B_NOPE + half:B_NOPE + 2 * half])
    return z.at[..., B_NOPE + half:B_NOPE + 2 * half].set(w[..., B_NOPE:B_NOPE + half])


def _mla_prep_kernel(cq_ref, lat_ref, kpe_ref, kpesw_ref, gqa_ref, gkva_ref, wq_ref, wqsw_ref, wk_ref, wv_ref,
                     gq_ref, gqsw_ref, gk_ref, gksw_ref, cos_ref, sin_ref, qadd_ref, kadd_ref, vadd_ref,
                     wvt_ref, vaddt_ref, q_ref, k_ref, v_ref, vt_ref):
    cosf, sinf = cos_ref[...], sin_ref[...]
    qscale = LOG2E * B_QK ** -0.5
    g1q, g2q = gq_ref[...] * cosf * qscale, gqsw_ref[...] * sinf * qscale
    g1k, g2k = gk_ref[...] * cosf, gksw_ref[...] * sinf

    def inv_rms(x):
        return lax.rsqrt(jnp.sum(x * x, axis=-1, keepdims=True) * (1.0 / B_QK) + EPS)

    cq = _rms(cq_ref[...].astype(F32), gqa_ref[...]).astype(BF16)
    qall = _mm(cq, wq_ref[...])
    qsw = _mm(cq, wqsw_ref[...])
    c = _rms(lat_ref[...].astype(F32), gkva_ref[...]).astype(BF16)
    kall = _mm(c, wk_ref[...])
    v_ref[...] = (_mm(c, wv_ref[...]) + vadd_ref[...]).astype(BF16)
    vt_ref[...] = (_nt(wvt_ref[...], c) + vaddt_ref[...]).astype(BF16)
    kpe = kpe_ref[...].astype(F32)
    krot = kpesw_ref[...].astype(F32) * g2k
    qadd, kadd = qadd_ref[...], kadd_ref[...]
    for h in range(B_HEADS):
        sl = slice(h * PAD_HEAD, (h + 1) * PAD_HEAD)
        x = qall[:, sl]
        q_ref[:, sl] = ((x * g1q + qsw[:, sl] * g2q) * inv_rms(x) + qadd).astype(BF16)
        y = kall[:, sl] + kpe
        k_ref[:, sl] = ((y * g1k + krot) * inv_rms(y) + kadd).astype(BF16)


def _mla_prep(proj, lp, tables, S):
    T = proj.shape[0]
    tm = TOKEN_TILE
    nper = S // tm
    HW = B_HEADS * PAD_HEAD
    cst = lambda i: (0, 0)
    tab = pl.BlockSpec((tm, PAD_HEAD), lambda i: (i % nper, 0))
    row = pl.BlockSpec((1, PAD_HEAD), cst)
    col = lambda off: pl.BlockSpec((tm, PAD_HEAD), lambda i: (i, off // PAD_HEAD))
    out = jax.ShapeDtypeStruct((T, HW), BF16)
    return pl.pallas_call(
        _mla_prep_kernel,
        grid=(T // tm,),
        in_specs=[
            pl.BlockSpec((tm, 256), lambda i: (i, OFF_BCQ // 256)),
            col(OFF_BLAT), col(OFF_BKPE), col(OFF_BKPESW),
            pl.BlockSpec((1, B_Q_LORA), cst), pl.BlockSpec((1, B_KV_LORA), cst),
            pl.BlockSpec((B_Q_LORA, HW), cst), pl.BlockSpec((B_Q_LORA, HW), cst),
            pl.BlockSpec((B_KV_LORA, HW), cst), pl.BlockSpec((B_KV_LORA, HW), cst),
            row, row, row, row, tab, tab, row, row, pl.BlockSpec((1, HW), cst),
            pl.BlockSpec((HW, B_KV_LORA), cst), pl.BlockSpec((HW, 1), cst),
        ],
        out_specs=[pl.BlockSpec((tm, HW), lambda i: (i, 0))] * 3 + [pl.BlockSpec((HW, tm), lambda i: (i // nper, i % nper))],
        out_shape=[out, out, out, jax.ShapeDtypeStruct((T // S * HW, S), BF16)],
        compiler_params=_cparams(("parallel",)),
        name="mla_prep",
    )(proj, proj, proj, proj, lp['b_gqa'], lp['b_gkva'], lp['b_wq'], lp['b_wqsw'], lp['b_wk'], lp['b_wv'],
      lp['b_gq'], lp['b_gqsw'], lp['b_gk'], lp['b_gksw'], *tables, *lp['b_adds'], lp['b_wv'].T, lp['b_adds'][2].T)


def _flash_kernel(q_ref, k_ref, v_ref, o_ref, *, S, online):
    tq, tk = FLASH_TQ, FLASH_TK
    nk = S // tk
    heads = [slice(hh * PAD_HEAD, (hh + 1) * PAD_HEAD) for hh in range(2)]
    qs = [q_ref[:, sl] for sl in heads]
    a0 = jnp.zeros((PAD_HEAD, tq), F32) if not online else jnp.zeros((tq, PAD_HEAD), F32)

    def probs(j, hh):
        r0 = pl.multiple_of(j * tk, tk)
        return jnp.exp2(_nt(k_ref[pl.ds(r0, tk), heads[hh]], qs[hh])).astype(BF16)

    def pv(j, hh, p):
        r0 = pl.multiple_of(j * tk, tk)
        return _mm(v_ref[heads[hh], pl.ds(r0, tk)], p)

    if online:
        def body(j, carry):
            r0 = pl.multiple_of(j * tk, tk)
            out = []
            for hh, sl in enumerate(heads):
                s = _nt(qs[hh], k_ref[pl.ds(r0, tk), sl])
                m, acc = carry[hh]
                mn = jnp.maximum(m, jnp.max(s, axis=-1, keepdims=True))
                out.append((mn, acc * jnp.exp2(m - mn) + _mm(jnp.exp2(s - mn).astype(BF16), v_ref[pl.ds(r0, tk), sl])))
            return tuple(out)

        init = (jnp.full((tq, 1), MASK_VALUE, F32), a0)
        res = lax.fori_loop(0, nk, body, (init, init))
        accs = [res[0][1], res[1][1]]
    else:
        def body(j, carry):
            out = []
            for hh in range(2):
                acc, p_prev = carry[hh]
                out.append((acc + pv(j - 1, hh, p_prev), probs(j, hh)))
            return tuple(out)

        res = lax.fori_loop(1, nk, body, ((a0, probs(0, 0)), (a0, probs(0, 1))), unroll=FLASH_UNROLL)
        accs = [(res[hh][0] + pv(nk - 1, hh, res[hh][1])).T for hh in range(2)]
    for hh in range(2):
        acc = accs[hh]
        o_ref[:, hh * B_V:(hh + 1) * B_V] = (acc[:, :B_V] / acc[:, SUM_LANE:SUM_LANE + 1]).astype(BF16)


def _flash(q, k, v, B, S, online):
    T = B * S
    tq = FLASH_TQ
    nq = S // tq
    if online:
        vspec = pl.BlockSpec((S, 2 * PAD_HEAD), lambda b, h, i: (b, h))
    else:
        vspec = pl.BlockSpec((2 * PAD_HEAD, S), lambda b, h, i: (b * (B_HEADS // 2) + h, 0))
    return pl.pallas_call(
        functools.partial(_flash_kernel, S=S, online=online),
        grid=(B, B_HEADS // 2, nq),
        in_specs=[
            pl.BlockSpec((tq, 2 * PAD_HEAD), lambda b, h, i: (b * nq + i, h)),
            pl.BlockSpec((S, 2 * PAD_HEAD), lambda b, h, i: (b, h)),
            vspec,
        ],
        out_specs=pl.BlockSpec((tq, 2 * B_V), lambda b, h, i: (b * nq + i, h)),
        out_shape=jax.ShapeDtypeStruct((T, B_HEADS * B_V), BF16),
        compiler_params=_cparams(("parallel", "parallel", "parallel")),
        name="mla_flash_online" if online else "mla_flash",
    )(q, k, v)


def _hgrn_consts(L, reverse):
    levels = []
    m = 1
    while m < L:
        levels.append(m)
        m *= 2
    nl = len(levels)
    t = np.arange(L)
    wall = np.zeros((nl + 2, L, L), np.float32)
    right = np.zeros((nl, L, 1), np.float32)
    mask = np.zeros((nl + 1, L, L), np.float32)
    for li, m in enumerate(levels):
        blk = t // (2 * m)
        mid = blk * 2 * m + m
        isr = t >= mid
        for tt in range(L):
            if isr[tt]:
                wall[li, tt, mid[tt]:tt + 1] = 1.0
            else:
                wall[li, tt, tt + 1:mid[tt]] = 1.0
        right[li, :, 0] = isr
        mask[li] = (blk[:, None] == blk[None, :]) & isr[:, None] & (~isr[None, :])
    mask[nl] = np.eye(L)
    wall[nl] = np.tril(np.ones((L, L)))
    wall[nl + 1] = np.triu(np.ones((L, L)), 1)
    if reverse:
        wall = wall[:, ::-1, ::-1]
        right = right[:, ::-1]
        mask = mask[:, ::-1, ::-1]
    right = np.broadcast_to(right, (nl, L, C_DK))
    wall = wall.reshape((nl + 2) * L, L)
    return (jnp.asarray(np.ascontiguousarray(np.concatenate([wall, wall], axis=1)), BF16),
            jnp.asarray(np.ascontiguousarray(right)), jnp.asarray(np.ascontiguousarray(mask)), nl)


def _hgrn_direction(q_ref, z_ref, v_ref, lb, wall_ref, right_ref, mask_ref, st_ref, o_ref, *, L, nl, reverse):
    z = z_ref[...].astype(F32)
    sig = jax.nn.sigmoid(z)
    lf = jnp.log(jnp.maximum(lb + (1.0 - lb) * sig, TINY)) * LOG2E
    key = (1.0 - lb) * (1.0 - sig)
    qx = q_ref[...].astype(F32)
    qh = qx * jax.nn.sigmoid(qx)
    hi = lf.astype(BF16)
    lo = (lf - hi.astype(F32)).astype(BF16)
    e_all = jnp.exp2(_mm(wall_ref[...], jnp.concatenate([hi, lo], axis=0)))
    last = 0 if reverse else L - 1
    for h in range(C_HEADS):
        sl = slice(h * C_DK, (h + 1) * C_DK)
        qh_h, key_h, v_h = qh[:, sl], key[:, sl], v_ref[:, sl]
        scores = _nt(qh_h.astype(BF16), key_h.astype(BF16)) * mask_ref[nl]
        q_minus_k = qh_h - key_h
        for li in range(nl):
            x = ((key_h + right_ref[li] * q_minus_k) * e_all[li * L:(li + 1) * L, sl]).astype(BF16)
            scores = scores + _nt(x, x) * mask_ref[li]
        o = _mm(scores.astype(BF16), v_h)
        ep = e_all[nl * L:(nl + 1) * L, sl]
        er = e_all[(nl + 1) * L:(nl + 2) * L, sl]
        st = st_ref[h]
        o_ref[:, sl] = o + _nt((qh_h * ep).astype(BF16), st.astype(BF16))
        st_ref[h] = st * ep[last:last + 1] + _mm(v_h.T, (key_h * er).astype(BF16))


def _hgrn_kernel(qf_ref, zf_ref, vf_ref, qb_ref, zb_ref, vb_ref, lb_ref, wf_ref, rf_ref, mf_ref, wb_ref, rb_ref, mb_ref,
                 of_ref, ob_ref, st_ref, *, L, nl):
    @pl.when(pl.program_id(1) == 0)
    def _():
        st_ref[...] = jnp.zeros_like(st_ref)

    _hgrn_direction(qf_ref, zf_ref, vf_ref, lb_ref[0:1], wf_ref, rf_ref, mf_ref, st_ref.at[0], of_ref,
                    L=L, nl=nl, reverse=False)
    _hgrn_direction(qb_ref, zb_ref, vb_ref, lb_ref[1:2], wb_ref, rb_ref, mb_ref, st_ref.at[1], ob_ref,
                    L=L, nl=nl, reverse=True)


def _hgrn(proj, lb, consts_f, consts_b, B, S):
    T = B * S
    L = HGRN_CHUNK
    nc = S // L
    nl = consts_f[3]
    fwd = lambda b, c: b * nc + c
    bwd = lambda b, c: b * nc + nc - 1 - c
    pspec = lambda rows, off: pl.BlockSpec((L, 512), lambda b, c: (rows(b, c), off // 512))
    cst2 = lambda b, c: (0, 0)
    cst3 = lambda b, c: (0, 0, 0)
    cspecs = lambda cs: [pl.BlockSpec(cs[0].shape, cst2), pl.BlockSpec(cs[1].shape, cst3), pl.BlockSpec(cs[2].shape, cst3)]
    out = jax.ShapeDtypeStruct((T, 512), F32)
    return pl.pallas_call(
        functools.partial(_hgrn_kernel, L=L, nl=nl),
        grid=(B, nc),
        in_specs=[pspec(fwd, OFF_CQ), pspec(fwd, OFF_CFF), pspec(fwd, OFF_CI),
                  pspec(bwd, OFF_CQ), pspec(bwd, OFF_CFB), pspec(bwd, OFF_CI),
                  pl.BlockSpec((2, 512), cst2)] + cspecs(consts_f) + cspecs(consts_b),
        out_specs=[pl.BlockSpec((L, 512), lambda b, c: (fwd(b, c), 0)), pl.BlockSpec((L, 512), lambda b, c: (bwd(b, c), 0))],
        out_shape=[out, out],
        scratch_shapes=[pltpu.VMEM((2, C_HEADS, C_DV, C_DK), F32)],
        compiler_params=_cparams(("parallel", "arbitrary")),
        name="hgrn",
    )(proj, proj, proj, proj, proj, proj, lb, *consts_f[:3], *consts_b[:3])


def _fft_consts(S, B):
    N2 = FFT_N2
    N1 = S // N2
    nbat = math.gcd(B, max(1, FFT1_ROWS // N1))
    d = np.arange(D_GROUP_DIM)
    ang = 2.0 * np.pi * ((d[:, None] * d[None, :]) % D_GROUP_DIM) / D_GROUP_DIM
    wc = np.concatenate([np.cos(ang), -np.sin(ang)], axis=1) / math.sqrt(D_GROUP_DIM)
    a = np.arange(N1)
    ang1 = 2.0 * np.pi * ((a[:, None] * a[None, :]) % N1) / N1
    eye = np.eye(nbat)
    c1, s1 = np.kron(eye, np.cos(ang1)), np.kron(eye, np.sin(ang1))
    m1 = np.block([[c1, s1], [-s1, c1]])
    p1 = jnp.arange(N1, dtype=jnp.int32)[:, None, None]
    p2 = jnp.arange(N2, dtype=jnp.int32)[None, :, None]
    s2 = jnp.arange(N2, dtype=jnp.int32)[None, None, :]
    kk = (s2 * (p1 + N1 * p2)) % S
    th = kk.astype(F32) * (2.0 * math.pi / S)
    m2 = jnp.concatenate([jnp.cos(th), jnp.sin(th)], axis=2) * (1.0 / math.sqrt(S))
    return jnp.asarray(wc, BF16), jnp.asarray(m1, BF16), m2.astype(BF16), N1, nbat


def _fft1_kernel(u_ref, wc_ref, m1_ref, ar_ref, ai_ref, *, rows, ct):
    wc, m1 = wc_ref[...], m1_ref[...]
    for g in range(ct // LANES):
        sl = slice(g * LANES, (g + 1) * LANES)
        z = _mm(u_ref[:, sl], wc)
        zs = jnp.concatenate([z[:, :LANES], z[:, LANES:]], axis=0).astype(BF16)
        a = _mm(m1, zs)
        ar_ref[:, sl] = a[:rows].astype(BF16)
        ai_ref[:, sl] = a[rows:].astype(BF16)


def _fft2_kernel(ar_ref, ai_ref, m2_ref, o_ref):
    for p in range(FFT2_P):
        x = jnp.concatenate([ar_ref[p], ai_ref[p]], axis=0)
        o_ref[:, p * 512:(p + 1) * 512] = _mm(m2_ref[p], x).astype(BF16)


def _fourier(u, consts, B, S):
    wc, m1, m2, N1, nbat = consts
    N2 = FFT_N2
    ct = FFT1_COLS
    P = FFT2_P
    W = N2 * 512
    rows = nbat * N1
    u2 = u.reshape(B * N1, W)
    sds = jax.ShapeDtypeStruct((B * N1, W), BF16)
    ar, ai = pl.pallas_call(
        functools.partial(_fft1_kernel, rows=rows, ct=ct),
        grid=(B // nbat, W // ct),
        in_specs=[
            pl.BlockSpec((rows, ct), lambda b, j: (b, j)),
            pl.BlockSpec(wc.shape, lambda b, j: (0, 0)),
            pl.BlockSpec(m1.shape, lambda b, j: (0, 0)),
        ],
        out_specs=[pl.BlockSpec((rows, ct), lambda b, j: (b, j))] * 2,
        out_shape=[sds, sds],
        compiler_params=_cparams(("parallel", "parallel")),
        name="fft_stage1",
    )(u2, wc, m1)
    ar3 = ar.reshape(B * N1, N2, 512)
    ai3 = ai.reshape(B * N1, N2, 512)
    y = pl.pallas_call(
        _fft2_kernel,
        grid=(B, N1 // P),
        in_specs=[
            pl.BlockSpec((P, N2, 512), lambda b, p: (b * (N1 // P) + p, 0, 0)),
            pl.BlockSpec((P, N2, 512), lambda b, p: (b * (N1 // P) + p, 0, 0)),
            pl.BlockSpec((P, N2, 2 * N2), lambda b, p: (p, 0, 0)),
        ],
        out_specs=pl.BlockSpec((N2, P * 512), lambda b, p: (b, p)),
        out_shape=jax.ShapeDtypeStruct((B * N2, N1 * 512), BF16),
        compiler_params=_cparams(("parallel", "parallel")),
        name="fft_stage2",
    )(ar3, ai3, m2)
    return y.reshape(B * S, 512)


def _merge_kernel(x_ref, oa_ref, ob_ref, cf_ref, cb_ref, cg_ref, od_ref, g1_ref, gn_ref, wg_ref, wb_ref, wo_ref, g2_ref,
                  wr_ref, x1_ref, h2_ref, aff_ref):
    x = x_ref[...]
    h = _rms(x, g1_ref[...]).astype(BF16)
    tot = cf_ref[...] + cb_ref[...]
    gate = cg_ref[...].astype(F32)
    gate = gate * jax.nn.sigmoid(gate)
    o_c = jnp.concatenate([_rms(tot[:, hd * C_DV:(hd + 1) * C_DV], gn_ref[...]) for hd in range(C_HEADS)], axis=1)
    branches = (oa_ref[...], ob_ref[...], (o_c * gate).astype(BF16), od_ref[...])
    mixed = None
    for n in range(N_BRANCHES):
        term = jax.nn.sigmoid(_mm(h, wg_ref[n])) * _mm(branches[n], wb_ref[n])
        mixed = term if mixed is None else mixed + term
    x1 = x + _mm(mixed.astype(BF16), wo_ref[...])
    x1_ref[...] = x1
    h2 = _rms(x1, g2_ref[...]).astype(BF16)
    h2_ref[...] = h2
    logits = _nt(wr_ref[...], h2)
    mx = jnp.max(logits, axis=0, keepdims=True)
    ex = jnp.exp(logits - mx)
    aff_ref[...] = ex / jnp.sum(ex, axis=0, keepdims=True)


def _merge(x2, oa, ob, cf, cb, proj, od, g1, gn, wg, wb, wo, g2, wr_t):
    T = x2.shape[0]
    tm = TOKEN_TILE
    tok = lambda w: pl.BlockSpec((tm, w), lambda i: (i, 0))
    c2 = lambda i: (0, 0)
    c3 = lambda i: (0, 0, 0)
    return pl.pallas_call(
        _merge_kernel,
        grid=(T // tm,),
        in_specs=[
            tok(D_MODEL), tok(512), tok(512), tok(512), tok(512),
            pl.BlockSpec((tm, 512), lambda i: (i, OFF_CG // 512)), tok(512),
            pl.BlockSpec((1, D_MODEL), c2), pl.BlockSpec((1, C_DV), c2),
            pl.BlockSpec((N_BRANCHES, D_MODEL, D_MODEL), c3),
            pl.BlockSpec((N_BRANCHES, BRANCH_WIDTH, D_MODEL), c3),
            pl.BlockSpec((D_MODEL, D_MODEL), c2),
            pl.BlockSpec((1, D_MODEL), c2),
            pl.BlockSpec((N_EXPERTS, D_MODEL), c2),
        ],
        out_specs=[tok(D_MODEL), tok(D_MODEL), pl.BlockSpec((N_EXPERTS, tm), lambda i: (0, i))],
        out_shape=[jax.ShapeDtypeStruct((T, D_MODEL), F32), jax.ShapeDtypeStruct((T, D_MODEL), BF16),
                   jax.ShapeDtypeStruct((N_EXPERTS, T), F32)],
        compiler_params=_cparams(("parallel",)),
        name="merge",
    )(x2, oa, ob, cf, cb, proj, od, g1, gn, wg, wb, wo, g2, wr_t)


def _cap_rows(T):
    cap = CAPACITY_FACTOR * T // N_EXPERTS
    ntiles = T // MOE_TILE
    need = cap + BF16_SUBLANES * ntiles + max(FFN_ROWS, MOE_SUB)
    return cap, ntiles, -(-need // FFN_ROWS) * FFN_ROWS


def _route_kernel(aff_ref, tri_ref, rank_ref, cnt_ref, off_ref, *, T, cap, ntiles):
    tk = MOE_TILE
    bits = pltpu.bitcast(aff_ref[...], jnp.int32)
    capf = jnp.float32(cap)

    def count(pred):
        return jnp.sum(jnp.where(pred, 1.0, 0.0), axis=-1, keepdims=True)

    def vbody(i, v):
        cand = v | lax.shift_left(jnp.int32(1), 30 - i)
        return jnp.where(count(bits >= cand) >= capf, cand, v)

    v = lax.fori_loop(0, 31, vbody, jnp.zeros((N_EXPERTS, 1), jnp.int32))
    need = capf - count(bits > v)
    ties = bits == v
    idx = lax.broadcasted_iota(jnp.int32, (N_EXPERTS, T), 1)
    nbits = max(1, (T - 1).bit_length())

    def jbody(i, j0):
        cand = j0 | lax.shift_left(jnp.int32(1), nbits - 1 - i)
        return jnp.where(count(ties & (idx < cand)) < need, cand, j0)

    j0 = lax.fori_loop(0, nbits, jbody, jnp.zeros((N_EXPERTS, 1), jnp.int32))
    sel = (bits > v) | (ties & (idx <= j0))
    tri = tri_ref[...]
    cnt_ref[...] = jnp.zeros_like(cnt_ref)
    for i in range(ntiles):
        sl = slice(i * tk, (i + 1) * tk)
        s_i = sel[:, sl]
        sf = jnp.where(s_i, 1.0, 0.0)
        r = _mm(sf.astype(BF16), tri)
        rank_ref[:, sl] = jnp.where(s_i, r, -1.0)
        cnt_ref[:, i:i + 1] = jnp.sum(sf, axis=-1, keepdims=True)
    cnt = cnt_ref[...]
    units = jnp.ceil(cnt * (1.0 / BF16_SUBLANES))
    off_ref[...] = (_mm(units.astype(BF16), tri[:LANES, :LANES]) * BF16_SUBLANES).astype(jnp.int32)


def _route(aff_t, tri, T):
    cap, ntiles, _ = _cap_rows(T)
    assert ntiles <= LANES and MOE_TILE // BF16_SUBLANES <= 256
    full = lambda s: pl.BlockSpec(s, lambda: tuple(0 for _ in s))
    rank, cnt, off = pl.pallas_call(
        functools.partial(_route_kernel, T=T, cap=cap, ntiles=ntiles),
        in_specs=[full((N_EXPERTS, T)), full(tri.shape)],
        out_specs=[full((N_EXPERTS, T)), full((N_EXPERTS, LANES)), full((N_EXPERTS, LANES))],
        out_shape=[jax.ShapeDtypeStruct((N_EXPERTS, T), F32), jax.ShapeDtypeStruct((N_EXPERTS, LANES), F32),
                   jax.ShapeDtypeStruct((N_EXPERTS, LANES), jnp.int32)],
        compiler_params=_cparams(()),
        name="moe_route",
    )(aff_t, tri)
    return rank, cnt.astype(jnp.int32), off


def _gather_kernel(cnt_ref, off_ref, rank_ref, h_ref, xe_ref, stage_ref, extra_ref, zero_ref, sem, xsem, *, cap):
    i, n = pl.program_id(0), pl.num_programs(0)
    R = MOE_SUB
    slot = i % 2
    h = h_ref[...]
    jrow = lax.broadcasted_iota(jnp.int32, (R, MOE_TILE), 0).astype(F32)
    zrows = zero_ref.shape[0]
    nzero = (xe_ref.shape[1] - cap) // zrows

    @pl.when(i == 0)
    def _():
        zero_ref[...] = jnp.zeros_like(zero_ref)

        def zero_copy(e, k):
            return pltpu.make_async_copy(zero_ref, xe_ref.at[e, pl.ds(cap + k * zrows, zrows)], sem.at[0, e])

        for k in range(nzero):
            for e in range(N_EXPERTS):
                zero_copy(e, k).start()
            for e in range(N_EXPERTS):
                zero_copy(e, k).wait()

    def rows(e, r):
        hit = rank_ref[e:e + 1, :] == jrow + (r * R).astype(F32)
        return _mm(jnp.where(hit, 1.0, 0.0).astype(BF16), h).astype(BF16)

    def out_copy(step, sl, e):
        r0 = pl.multiple_of(off_ref[e, step], BF16_SUBLANES)
        return pltpu.make_async_copy(stage_ref.at[sl, e], xe_ref.at[e, pl.ds(r0, R)], sem.at[sl, e])

    for e in range(N_EXPERTS):
        stage_ref[slot, e] = rows(e, jnp.int32(0))

    @pl.when(i > 0)
    def _():
        for e in range(N_EXPERTS):
            out_copy(i - 1, 1 - slot, e).wait()

    for e in range(N_EXPERTS):
        out_copy(i, slot, e).start()

    for e in range(N_EXPERTS):
        def body(r, carry, e=e):
            extra_ref[...] = rows(e, r)
            r0 = pl.multiple_of(off_ref[e, i] + r * R, BF16_SUBLANES)
            cp = pltpu.make_async_copy(extra_ref, xe_ref.at[e, pl.ds(r0, R)], xsem.at[0])
            cp.start()
            cp.wait()
            return carry

        lax.fori_loop(1, (cnt_ref[e, i] + R - 1) // R, body, 0)

    @pl.when(i == n - 1)
    def _():
        for e in range(N_EXPERTS):
            out_copy(i, slot, e).wait()


def _gather(cnt, off, rank, h2, T):
    cap, ntiles, rows = _cap_rows(T)
    tk = MOE_TILE
    R = MOE_SUB
    return pl.pallas_call(
        functools.partial(_gather_kernel, cap=cap),
        grid_spec=pltpu.PrefetchScalarGridSpec(
            num_scalar_prefetch=2,
            grid=(ntiles,),
            in_specs=[
                pl.BlockSpec((N_EXPERTS, tk), lambda i, c, o: (0, i)),
                pl.BlockSpec((tk, D_MODEL), lambda i, c, o: (i, 0)),
            ],
            out_specs=pl.BlockSpec(memory_space=pl.ANY),
            scratch_shapes=[
                pltpu.VMEM((2, N_EXPERTS, R, D_MODEL), BF16),
                pltpu.VMEM((R, D_MODEL), BF16),
                pltpu.VMEM((math.gcd(rows - cap, FFN_ROWS), D_MODEL), BF16),
                pltpu.SemaphoreType.DMA((2, N_EXPERTS)),
                pltpu.SemaphoreType.DMA((1,)),
            ],
        ),
        out_shape=jax.ShapeDtypeStruct((N_EXPERTS, rows, D_MODEL), BF16),
        compiler_params=_cparams(("arbitrary",)),
        name="moe_gather",
    )(cnt, off, rank, h2)


def _ffn_kernel(used_ref, x_ref, wg_ref, wu_ref, wd_ref, y_ref, wg_s, wu_s, wd_s):
    live = pl.program_id(1) * FFN_ROWS < used_ref[pl.program_id(0)]

    @pl.when(pl.program_id(1) == 0)
    def _():
        wg_s[...] = wg_ref[0].astype(BF16)
        wu_s[...] = wu_ref[0].astype(BF16)
        wd_s[...] = wd_ref[0].astype(BF16)

    @pl.when(live)
    def _():
        x = x_ref[0]
        a = _mm(x, wg_s[...])
        u = _mm(x, wu_s[...])
        hmid = (a * jax.nn.sigmoid(a) * u).astype(BF16)
        y_ref[0] = _mm(hmid, wd_s[...]).astype(BF16)

    @pl.when(jnp.logical_not(live))
    def _():
        y_ref[...] = jnp.zeros_like(y_ref)


def _ffn(used, xe, wg, wu, wd, layer):
    E, rows, _ = xe.shape
    wspec = lambda s: pl.BlockSpec((None, 1) + s, lambda e, j, u: (layer, e, 0, 0))
    return pl.pallas_call(
        _ffn_kernel,
        grid_spec=pltpu.PrefetchScalarGridSpec(
            num_scalar_prefetch=1,
            grid=(E, rows // FFN_ROWS),
            in_specs=[
                pl.BlockSpec((1, FFN_ROWS, D_MODEL), lambda e, j, u: (e, j, 0)),
                wspec((D_MODEL, EXPERT_FF)), wspec((D_MODEL, EXPERT_FF)), wspec((EXPERT_FF, D_MODEL)),
            ],
            out_specs=pl.BlockSpec((1, FFN_ROWS, D_MODEL), lambda e, j, u: (e, j, 0)),
            scratch_shapes=[pltpu.VMEM((D_MODEL, EXPERT_FF), BF16), pltpu.VMEM((D_MODEL, EXPERT_FF), BF16),
                            pltpu.VMEM((EXPERT_FF, D_MODEL), BF16)],
        ),
        out_shape=jax.ShapeDtypeStruct(xe.shape, BF16),
        compiler_params=_cparams(("parallel", "arbitrary")),
        name="moe_ffn",
    )(used, xe, wg, wu, wd)


def _scatter_kernel(cnt_ref, off_ref, x_ref, rank_ref, aff_ref, ye_ref, o_ref, seg_ref, oh_ref, extra_ref, sem, xsem):
    i, n = pl.program_id(0), pl.num_programs(0)
    R = MOE_SUB
    tk = MOE_TILE
    slot = i % 2

    def seg_copy(step, sl, e):
        r0 = pl.multiple_of(off_ref[e, step], BF16_SUBLANES)
        return pltpu.make_async_copy(ye_ref.at[e, pl.ds(r0, R)], seg_ref.at[sl, pl.ds(e * R, R)], sem.at[sl, e])

    @pl.when(i == 0)
    def _():
        for e in range(N_EXPERTS):
            seg_copy(i, slot, e).start()

    @pl.when(i + 1 < n)
    def _():
        for e in range(N_EXPERTS):
            seg_copy(i + 1, 1 - slot, e).start()

    jlane = lax.broadcasted_iota(jnp.int32, (tk, R), 1).astype(F32)
    for e in range(N_EXPERTS):
        oh_ref[:, e * R:(e + 1) * R] = jnp.where(rank_ref[:, e:e + 1] == jlane, aff_ref[:, e:e + 1], 0.0).astype(BF16)
    for e in range(N_EXPERTS):
        seg_copy(i, slot, e).wait()
    o_ref[...] = x_ref[...] + _mm(oh_ref[...], seg_ref[slot])

    for e in range(N_EXPERTS):
        def body(r, carry, e=e):
            r0 = pl.multiple_of(off_ref[e, i] + r * R, BF16_SUBLANES)
            cp = pltpu.make_async_copy(ye_ref.at[e, pl.ds(r0, R)], extra_ref, xsem.at[0])
            cp.start()
            hit = rank_ref[:, e:e + 1] == jlane + (r * R).astype(F32)
            w = jnp.where(hit, aff_ref[:, e:e + 1], 0.0).astype(BF16)
            cp.wait()
            o_ref[...] += _mm(w, extra_ref[...])
            return carry

        lax.fori_loop(1, (cnt_ref[e, i] + R - 1) // R, body, 0)


def _scatter(cnt, off, x1, rank_t, aff_tok, ye, T):
    _, ntiles, rows = _cap_rows(T)
    tk = MOE_TILE
    R = MOE_SUB
    return pl.pallas_call(
        _scatter_kernel,
        grid_spec=pltpu.PrefetchScalarGridSpec(
            num_scalar_prefetch=2,
            grid=(ntiles,),
            in_specs=[
                pl.BlockSpec((tk, D_MODEL), lambda i, c, o: (i, 0)),
                pl.BlockSpec((tk, N_EXPERTS), lambda i, c, o: (i, 0)),
                pl.BlockSpec((tk, N_EXPERTS), lambda i, c, o: (i, 0)),
                pl.BlockSpec(memory_space=pl.ANY),
            ],
            out_specs=pl.BlockSpec((tk, D_MODEL), lambda i, c, o: (i, 0)),
            scratch_shapes=[
                pltpu.VMEM((2, N_EXPERTS * R, D_MODEL), BF16),
                pltpu.VMEM((tk, N_EXPERTS * R), BF16),
                pltpu.VMEM((R, D_MODEL), BF16),
                pltpu.SemaphoreType.DMA((2, N_EXPERTS)),
                pltpu.SemaphoreType.DMA((1,)),
            ],
        ),
        out_shape=jax.ShapeDtypeStruct((T, D_MODEL), F32),
        compiler_params=_cparams(("arbitrary",)),
        name="moe_scatter",
    )(cnt, off, x1, rank_t, aff_tok, ye)


def _expert_choice(x1, h2, aff_t, tri, wg, wu, wd, layer):
    T = x1.shape[0]
    rank, cnt, off = _route(aff_t, tri, T)
    ntiles = T // MOE_TILE
    used = off[:, ntiles - 1] + -(-cnt[:, ntiles - 1] // BF16_SUBLANES) * BF16_SUBLANES
    xe = _gather(cnt, off, rank, h2, T)
    ye = _ffn(used, xe, wg, wu, wd, layer)
    return _scatter(cnt, off, x1, rank.T, aff_t.T, ye, T)


def _pad_heads(w, n_heads, width, pad_to):
    k = w.shape[0]
    w = w.reshape(k, n_heads, width)
    return jnp.pad(w, ((0, 0), (0, 0), (0, pad_to - width))).reshape(k, n_heads * pad_to)


def _pack_layer(l, p):
    w_in = p['w_in'][l]
    edges = np.cumsum([0, 512, 128, 128, B_Q_LORA, B_KV_LORA + B_ROPE, 512, 512, 512, 512, 512, 512])
    a_q, a_k, a_v, b_cq, b_ckv, c_q, c_ff, c_fb, c_i, c_g, d_u = [w_in[:, edges[j]:edges[j + 1]] for j in range(11)]
    zeros = lambda n: jnp.zeros((D_MODEL, n), w_in.dtype)
    b_kpe = jnp.concatenate([zeros(B_NOPE), b_ckv[:, B_KV_LORA:], zeros(PAD_HEAD - B_QK)], axis=1)
    w_main = jnp.concatenate([a_q, c_q, c_ff, c_fb, c_i, c_g, a_k, a_v, b_cq, b_ckv[:, :B_KV_LORA], b_kpe,
                              _swap_rope_halves(b_kpe)], axis=1).astype(BF16)
    kv_up = p['b_kv_up'][l].reshape(B_KV_LORA, B_HEADS, B_NOPE + B_V)
    pad_g = lambda g: jnp.pad(g, (0, PAD_HEAD - B_QK)).reshape(1, PAD_HEAD)
    b_wq = _pad_heads(p['b_q_up'][l], B_HEADS, B_QK, PAD_HEAD)
    b_wqsw = _swap_rope_halves(b_wq.reshape(B_Q_LORA, B_HEADS, PAD_HEAD)).reshape(B_Q_LORA, B_HEADS * PAD_HEAD)
    b_gq, b_gk = pad_g(p['b_q_norm_g'][l]), pad_g(p['b_k_norm_g'][l])
    sink = p['a_sink'][l].astype(F32).reshape(A_KV_HEADS, A_HEADS // A_KV_HEADS)
    b_bound = math.sqrt(B_QK) * jnp.max(jnp.abs(p['b_q_norm_g'][l])) * jnp.max(jnp.abs(p['b_k_norm_g'][l]))
    lane = jnp.arange(PAD_HEAD)
    b_qadd = jnp.where(lane == SCORE_LANE, 1.0, 0.0).astype(F32).reshape(1, PAD_HEAD)
    b_kadd = jnp.where(lane == SCORE_LANE, -LOG2E * b_bound, 0.0).astype(F32).reshape(1, PAD_HEAD)
    b_vadd = jnp.tile(jnp.where(lane == SUM_LANE, 1.0, 0.0).astype(F32), B_HEADS).reshape(1, B_HEADS * PAD_HEAD)
    a_bound, a_bias, a_sink_term = _win_fast_tables(p['a_q_norm_g'][l], p['a_k_norm_g'][l], p['a_sink'][l])
    return dict(
        a_bound=a_bound, a_bias=a_bias, a_sink_term=a_sink_term,
        a_gq_t=(jnp.tile(p['a_q_norm_g'][l], A_HEADS) * (LOG2E * A_HEAD_DIM ** -0.5)).reshape(1, A_HEADS * A_HEAD_DIM),
        a_gk_t=jnp.tile(p['a_k_norm_g'][l], A_KV_HEADS).reshape(1, A_KV_HEADS * A_HEAD_DIM),
        b_bound=b_bound, b_adds=(b_qadd, b_kadd, b_vadd),
        g_mix=p['norm_mix_g'][l].reshape(1, D_MODEL), w_main=w_main, w_u=d_u.astype(BF16),
        a_gq=p['a_q_norm_g'][l].reshape(1, A_HEAD_DIM), a_gk=p['a_k_norm_g'][l].reshape(1, A_HEAD_DIM),
        a_sink=jnp.repeat(sink, WINDOW, axis=1)[:, :, None],
        b_gqa=p['b_q_a_norm_g'][l].reshape(1, B_Q_LORA), b_gkva=p['b_kv_a_norm_g'][l].reshape(1, B_KV_LORA),
        b_wq=b_wq.astype(BF16), b_wqsw=b_wqsw.astype(BF16),
        b_wk=_pad_heads(kv_up[:, :, :B_NOPE].reshape(B_KV_LORA, -1), B_HEADS, B_NOPE, PAD_HEAD).astype(BF16),
        b_wv=_pad_heads(kv_up[:, :, B_NOPE:].reshape(B_KV_LORA, -1), B_HEADS, B_V, PAD_HEAD).astype(BF16),
        b_gq=b_gq, b_gk=b_gk, b_gqsw=_swap_rope_halves(b_gq), b_gksw=_swap_rope_halves(b_gk),
        c_gn=p['c_out_norm_g'][l].reshape(1, C_DV),
        w_gate=p['w_gate'][l].astype(BF16), w_branch=p['w_branch'][l].astype(BF16), w_out=p['w_out'][l].astype(BF16),
        g_ffn=p['norm_ffn_g'][l].reshape(1, D_MODEL), w_router_t=p['w_router'][l].T.astype(BF16),
        layer=l, w_e_gate=p['w_e_gate'], w_e_up=p['w_e_up'], w_e_down=p['w_e_down'],
    )


def _layer(x2, B, S, lp, lb, cst):
    proj, u = _in_proj(x2, lp['g_mix'], lp['w_main'], lp['w_u'])
    o_a = lax.cond(lp['a_bound'] <= FAST_SOFTMAX_MAX_BOUND,
                   lambda pr: _win_attn_fast(pr, lp['a_gq_t'], lp['a_gk_t'], cst['win'], lp['a_bias'], lp['a_sink_term'], B, S),
                   lambda pr: _win_attn(pr, lp['a_gq'], lp['a_gk'], lp['a_sink'], B, S), proj)
    q, k, v, v_t = _mla_prep(proj, lp, cst['rope'][S], S)
    o_b = lax.cond(lp['b_bound'] <= FAST_SOFTMAX_MAX_BOUND,
                   lambda q, k, v, v_t: _flash(q, k, v_t, B, S, False),
                   lambda q, k, v, v_t: _flash(q, k, v, B, S, True), q, k, v, v_t)
    c_f, c_b = _hgrn(proj, lb, cst['hgrn_f'], cst['hgrn_b'], B, S)
    o_d = _fourier(u, cst['fft'][S], B, S)
    x1, h2, aff_t = _merge(x2, o_a, o_b, c_f, c_b, proj, o_d, lp['g_mix'], lp['c_gn'], lp['w_gate'], lp['w_branch'],
                           lp['w_out'], lp['g_ffn'], lp['w_router_t'])
    return _expert_choice(x1, h2, aff_t, cst['tri'], lp['w_e_gate'], lp['w_e_up'], lp['w_e_down'], lp['layer'])


def kernel(x_prompt, x_sample, norm_mix_g, w_in, a_q_norm_g, a_k_norm_g, a_sink, b_q_a_norm_g, b_q_up,
           b_kv_a_norm_g, b_kv_up, b_q_norm_g, b_k_norm_g, c_lb_logits, c_out_norm_g, w_branch, w_gate, w_out,
           norm_ffn_g, w_router, w_e_gate, w_e_up, w_e_down):
    p = dict(norm_mix_g=norm_mix_g, w_in=w_in, a_q_norm_g=a_q_norm_g, a_k_norm_g=a_k_norm_g, a_sink=a_sink,
             b_q_a_norm_g=b_q_a_norm_g, b_q_up=b_q_up, b_kv_a_norm_g=b_kv_a_norm_g, b_kv_up=b_kv_up,
             b_q_norm_g=b_q_norm_g, b_k_norm_g=b_k_norm_g, c_out_norm_g=c_out_norm_g, w_branch=w_branch,
             w_gate=w_gate, w_out=w_out, norm_ffn_g=norm_ffn_g, w_router=w_router, w_e_gate=w_e_gate,
             w_e_up=w_e_up, w_e_down=w_e_down)
    lb_w = jax.nn.softmax(c_lb_logits.astype(F32), axis=0)
    lb_all = jnp.cumsum(lb_w, axis=0) - lb_w[0:1]
    groups = [x_prompt, x_sample]
    seqs = sorted({g.shape[1] for g in groups})
    tk = MOE_TILE
    cst = dict(
        rope={S: _rope_tables(S) for S in seqs},
        fft={g.shape[1]: _fft_consts(g.shape[1], g.shape[0]) for g in groups},
        hgrn_f=_hgrn_consts(HGRN_CHUNK, False), hgrn_b=_hgrn_consts(HGRN_CHUNK, True),
        win=_win_fast_consts(),
        tri=jnp.asarray(np.triu(np.ones((tk, tk), np.float32), 1), BF16),
    )
    layers = [_pack_layer(l, p) for l in range(DEPTH)]
    outs = []
    for xg in groups:
        B, S, _ = xg.shape
        y = xg.reshape(B * S, D_MODEL)
        for l in range(DEPTH):
            y = _layer(y, B, S, layers[l], lb_all[l], cst)
        outs.append(y.reshape(B, S, D_MODEL))
    return tuple(outs)
```

```python
import functools
import math

import numpy as np
import jax
import jax.numpy as jnp
from jax import lax
from jax.experimental import pallas as pl
from jax.experimental.pallas import tpu as pltpu

F32 = jnp.float32
BF16 = jnp.bfloat16

D_MODEL = 1024
DEPTH = 2
A_HEADS, A_KV_HEADS, A_HEAD_DIM, WINDOW = 8, 2, 64, 128
B_HEADS, B_Q_LORA, B_KV_LORA, B_NOPE, B_ROPE, B_V = 8, 256, 128, 64, 32, 64
B_QK = B_NOPE + B_ROPE
ROPE_THETA = 10000.0
C_HEADS, C_DK, C_DV = 4, 128, 128
D_GROUPS, D_GROUP_DIM = 4, 128
N_BRANCHES, BRANCH_WIDTH = 4, 512
N_EXPERTS, EXPERT_FF, CAPACITY_FACTOR = 16, 1024, 2
EPS = 1e-6
MASK_VALUE = -1e30
TINY = 1e-30
LOG2E = 1.4426950408889634

SCORE_LANE = B_QK
SUM_LANE = B_V
FAST_SOFTMAX_MAX_BOUND = 30.0

LANES = 128
BF16_SUBLANES = 16
V7X_VMEM_LIMIT = 56 * 1024 * 1024

TOKEN_TILE = 512
PAD_HEAD = 128
HGRN_CHUNK = 128
FFT_N2 = 128
FFT1_COLS = 4096
FFT1_ROWS = 64
FFT2_P = 4
FLASH_TQ, FLASH_TK, FLASH_UNROLL = 512, 512, 15
MOE_TILE = 512
MOE_SUB = 128
FFN_ROWS = 512

OFF_AQ, OFF_CQ, OFF_CFF, OFF_CFB, OFF_CI, OFF_CG = 0, 512, 1024, 1536, 2048, 2560
OFF_AK, OFF_AV, OFF_BCQ, OFF_BLAT, OFF_BKPE, OFF_BKPESW = 3072, 3200, 3328, 3584, 3712, 3840
PROJ_W = 3968


def _cparams(sem, vmem=V7X_VMEM_LIMIT):
    return pltpu.CompilerParams(dimension_semantics=sem, vmem_limit_bytes=vmem)


def _nt(a, b):
    return lax.dot_general(a, b, (((1,), (1,)), ((), ())), preferred_element_type=F32)


def _mm(a, b):
    return jnp.dot(a, b, preferred_element_type=F32)


def _rms(x, g):
    return x * lax.rsqrt(jnp.mean(x * x, axis=-1, keepdims=True) + EPS) * g


def _in_proj_kernel(x_ref, g_ref, w_ref, wu_ref, p_ref, u_ref):
    h = _rms(x_ref[...], g_ref[...]).astype(BF16)
    for c0 in range(0, PROJ_W, 512):
        c1 = min(c0 + 512, PROJ_W)
        p_ref[:, c0:c1] = _mm(h, w_ref[:, c0:c1]).astype(BF16)
    u_ref[...] = _mm(h, wu_ref[...]).astype(BF16)


def _in_proj(x2, g, w_main, w_u):
    T = x2.shape[0]
    tm = TOKEN_TILE
    return pl.pallas_call(
        _in_proj_kernel,
        grid=(T // tm,),
        in_specs=[
            pl.BlockSpec((tm, D_MODEL), lambda i: (i, 0)),
            pl.BlockSpec((1, D_MODEL), lambda i: (0, 0)),
            pl.BlockSpec((D_MODEL, PROJ_W), lambda i: (0, 0)),
            pl.BlockSpec((D_MODEL, 512), lambda i: (0, 0)),
        ],
        out_specs=[
            pl.BlockSpec((tm, PROJ_W), lambda i: (i, 0)),
            pl.BlockSpec((tm, 512), lambda i: (i, 0)),
        ],
        out_shape=[jax.ShapeDtypeStruct((T, PROJ_W), BF16), jax.ShapeDtypeStruct((T, 512), BF16)],
        compiler_params=_cparams(("parallel",)),
        name="in_proj",
    )(x2, g, w_main, w_u)


def _win_attn_kernel(q_ref, kp_ref, kc_ref, kn_ref, vp_ref, vc_ref, vn_ref, gq_ref, gk_ref, sink_ref, o_ref, *, nb):
    n = pl.program_id(1)
    W = WINDOW
    G = A_HEADS // A_KV_HEADS
    q = q_ref[...].astype(F32)
    kcat = jnp.concatenate([kp_ref[...], kc_ref[...], kn_ref[...]], axis=0).astype(F32)
    vcat = jnp.concatenate([vp_ref[...], vc_ref[...], vn_ref[...]], axis=0)
    gq = gq_ref[...]
    gk = gk_ref[...]
    row = lax.broadcasted_iota(jnp.int32, (G * W, 3 * W), 0)
    col = lax.broadcasted_iota(jnp.int32, (G * W, 3 * W), 1)
    qi = row % W
    dist = jnp.abs(col - W - qi)
    valid = dist <= W
    valid = valid & ((col >= W) | (n > 0)) & ((col < 2 * W) | (n < nb - 1))
    distf = dist.astype(F32)
    hrow = lax.broadcasted_iota(jnp.int32, (G * W, 1), 0) // W
    for g in range(A_KV_HEADS):
        qs = []
        for j in range(G):
            h = g * G + j
            qs.append(_rms(q[:, h * A_HEAD_DIM:(h + 1) * A_HEAD_DIM], gq))
        qg = (jnp.concatenate(qs, axis=0) * (A_HEAD_DIM ** -0.5)).astype(BF16)
        kg = _rms(kcat[:, g * A_HEAD_DIM:(g + 1) * A_HEAD_DIM], gk).astype(BF16)
        vg = vcat[:, g * A_HEAD_DIM:(g + 1) * A_HEAD_DIM]
        slope = jnp.exp2(-(hrow + (g * G + 1)).astype(F32))
        s = _nt(qg, kg) - slope * distf
        s = jnp.where(valid, s, MASK_VALUE)
        sk = sink_ref[g]
        m = jnp.maximum(jnp.max(s, axis=-1, keepdims=True), sk)
        p = jnp.exp(s - m)
        den = jnp.sum(p, axis=-1, keepdims=True) + jnp.exp(sk - m)
        o = _mm(p.astype(BF16), vg) / den
        for j in range(G):
            h = g * G + j
            o_ref[:, h * A_HEAD_DIM:(h + 1) * A_HEAD_DIM] = o[j * W:(j + 1) * W].astype(BF16)


def _win_fast_consts():
    d = A_HEAD_DIM
    lane = np.arange(A_HEADS * d)
    return jnp.asarray((lane[:, None] // d == lane[None, :] // d).astype(np.float32) / d, BF16)


def _win_fast_tables(gq, gk, sink):
    W = WINDOW
    bound = math.sqrt(A_HEAD_DIM) * jnp.max(jnp.abs(gq)) * jnp.max(jnp.abs(gk))
    sink = sink.astype(F32)
    m = jnp.maximum(bound, sink)
    slopes = 2.0 ** (-8.0 * jnp.arange(1, A_HEADS + 1, dtype=F32) / A_HEADS)
    r = jnp.arange(W)[:, None]
    c = jnp.arange(3 * W)[None, :]
    dist = jnp.abs(c - W - r)
    inside = dist <= W
    variants = []
    for first, last in ((False, False), (True, False), (False, True), (True, True)):
        valid = inside & ((c >= W) | (not first)) & ((c < 2 * W) | (not last))
        b = LOG2E * (-slopes[:, None, None] * dist[None].astype(F32) - m[:, None, None])
        variants.append(jnp.where(valid[None], b, MASK_VALUE))
    sink_term = jnp.exp2(LOG2E * (sink - m))
    return bound, jnp.stack(variants), jnp.broadcast_to(sink_term[:, None, None], (A_HEADS, 1, LANES))


def _win_fast_kernel(q_ref, kp_ref, kc_ref, kn_ref, vp_ref, vc_ref, vn_ref, gq_ref, gk_ref, bd_ref, bias_ref, st_ref,
                     o_ref):
    d = A_HEAD_DIM
    q = q_ref[...].astype(F32)
    bd = bd_ref[...]
    qf = q * lax.rsqrt(_mm((q * q).astype(BF16), bd) + EPS) * gq_ref[...]
    kcat = jnp.concatenate([kp_ref[...], kc_ref[...], kn_ref[...]], axis=0).astype(F32)
    kss = _mm((kcat * kcat).astype(BF16), bd[:LANES, :LANES])
    kn = kcat * lax.rsqrt(kss + EPS) * gk_ref[...]
    vcat = jnp.concatenate([vp_ref[...], vc_ref[...], vn_ref[...]], axis=0).astype(F32)
    ones = jnp.ones((3 * WINDOW, LANES), BF16)
    lane = lax.broadcasted_iota(jnp.int32, (1, LANES), 1)
    half = [jnp.where(lane < d, 1.0, 0.0), jnp.where(lane >= d, 1.0, 0.0)]
    kn_sw, v_sw = pltpu.roll(kn, d, 1), pltpu.roll(vcat, d, 1)
    G = A_HEADS // A_KV_HEADS
    for g in range(A_KV_HEADS):
        kk = jnp.where((lane < d) == (g == 0), kn, kn_sw).astype(BF16)
        rhs = [jnp.concatenate([((vcat if par == g else v_sw) * half[par]).astype(BF16), ones], axis=1)
               for par in range(2)]
        for jj in range(G // 2):
            pair = g * (G // 2) + jj
            qp = qf[:, pair * LANES:(pair + 1) * LANES]
            acc = None
            for par in range(2):
                h = 2 * pair + par
                s = _nt((qp * half[par]).astype(BF16), kk) + bias_ref[0, h]
                res = _mm(jnp.exp2(s).astype(BF16), rhs[par])
                o = res[:, :LANES] / (res[:, LANES:] + st_ref[h])
                acc = o if acc is None else acc + o
            o_ref[:, pair * LANES:(pair + 1) * LANES] = acc.astype(BF16)


def _win_attn_fast(proj, gq_t, gk_t, consts, bias, sink_term, B, S):
    T = B * S
    W = WINDOW
    nb = S // W
    kcol, vcol = OFF_AK // 128, OFF_AV // 128
    bd = consts
    cur = lambda b, n: b * nb + n
    prev = lambda b, n: b * nb + jnp.maximum(n - 1, 0)
    nxt = lambda b, n: b * nb + jnp.minimum(n + 1, nb - 1)
    spec = lambda rowf, c: pl.BlockSpec((W, 128), lambda b, n: (rowf(b, n), c))
    variant = lambda b, n: ((n == 0).astype(jnp.int32) + 2 * (n == nb - 1).astype(jnp.int32), 0, 0, 0)
    c2 = lambda b, n: (0, 0)
    c3 = lambda b, n: (0, 0, 0)
    return pl.pallas_call(
        _win_fast_kernel,
        grid=(B, nb),
        in_specs=[
            pl.BlockSpec((W, 512), lambda b, n: (cur(b, n), OFF_AQ // 512)),
            spec(prev, kcol), spec(cur, kcol), spec(nxt, kcol),
            spec(prev, vcol), spec(cur, vcol), spec(nxt, vcol),
            pl.BlockSpec((1, 512), c2), pl.BlockSpec((1, LANES), c2),
            pl.BlockSpec(bd.shape, c2),
            pl.BlockSpec((1, A_HEADS, W, 3 * W), variant),
            pl.BlockSpec((A_HEADS, 1, LANES), c3),
        ],
        out_specs=pl.BlockSpec((W, 512), lambda b, n: (cur(b, n), 0)),
        out_shape=jax.ShapeDtypeStruct((T, 512), BF16),
        compiler_params=_cparams(("parallel", "parallel")),
        name="win_attn_fast",
    )(proj, proj, proj, proj, proj, proj, proj, gq_t, gk_t, bd, bias, sink_term)


def _win_attn(proj, gq, gk, sink_cols, B, S):
    T = B * S
    W = WINDOW
    nb = S // W
    kcol, vcol = OFF_AK // 128, OFF_AV // 128

    def cur(b, n):
        return b * nb + n

    def prev(b, n):
        return b * nb + jnp.maximum(n - 1, 0)

    def nxt(b, n):
        return b * nb + jnp.minimum(n + 1, nb - 1)

    def spec(rowf, c):
        return pl.BlockSpec((W, 128), lambda b, n: (rowf(b, n), c))

    return pl.pallas_call(
        functools.partial(_win_attn_kernel, nb=nb),
        grid=(B, nb),
        in_specs=[
            pl.BlockSpec((W, 512), lambda b, n: (cur(b, n), OFF_AQ // 512)),
            spec(prev, kcol), spec(cur, kcol), spec(nxt, kcol),
            spec(prev, vcol), spec(cur, vcol), spec(nxt, vcol),
            pl.BlockSpec((1, A_HEAD_DIM), lambda b, n: (0, 0)),
            pl.BlockSpec((1, A_HEAD_DIM), lambda b, n: (0, 0)),
            pl.BlockSpec((A_KV_HEADS, 4 * W, 1), lambda b, n: (0, 0, 0)),
        ],
        out_specs=pl.BlockSpec((W, 512), lambda b, n: (cur(b, n), 0)),
        out_shape=jax.ShapeDtypeStruct((T, 512), BF16),
        compiler_params=_cparams(("parallel", "parallel")),
        name="win_attn",
    )(proj, proj, proj, proj, proj, proj, proj, gq, gk, sink_cols)


def _rope_tables(S):
    half = B_ROPE // 2
    inv = 1.0 / (ROPE_THETA ** (np.arange(half, dtype=np.float32) / half))
    ang = np.arange(S, dtype=np.float32)[:, None] * inv[None, :]
    cos, sin = np.cos(ang), np.sin(ang)
    cosf = np.ones((S, PAD_HEAD), np.float32)
    sinf = np.zeros((S, PAD_HEAD), np.float32)
    cosf[:, B_NOPE:B_NOPE + half] = cos
    cosf[:, B_NOPE + half:B_NOPE + 2 * half] = cos
    sinf[:, B_NOPE:B_NOPE + half] = -sin
    sinf[:, B_NOPE + half:B_NOPE + 2 * half] = sin
    return jnp.asarray(cosf), jnp.asarray(sinf)


def _swap_rope_halves(w):
    half = B_ROPE // 2
    z = jnp.zeros_like(w)
    z = z.at[..., B_NOPE:B_NOPE + half].set(w[..., B_NOPE + half:B_NOPE + 2 * half])
    return z.at[..., B_NOPE + half:B_NOPE + 2 * half].set(w[..., B_NOPE:B_NOPE + half])


def _mla_prep_kernel(cq_ref, lat_ref, kpe_ref, kpesw_ref, gqa_ref, gkva_ref, wq_ref, wqsw_ref, wk_ref, wv_ref,
                     gq_ref, gqsw_ref, gk_ref, gksw_ref, cos_ref, sin_ref, qadd_ref, kadd_ref, vadd_ref,
                     wvt_ref, vaddt_ref, q_ref, k_ref, v_ref, vt_ref):
    cosf, sinf = cos_ref[...], sin_ref[...]
    qscale = LOG2E * B_QK ** -0.5
    g1q, g2q = gq_ref[...] * cosf * qscale, gqsw_ref[...] * sinf * qscale
    g1k, g2k = gk_ref[...] * cosf, gksw_ref[...] * sinf

    def inv_rms(x):
        return lax.rsqrt(jnp.sum(x * x, axis=-1, keepdims=True) * (1.0 / B_QK) + EPS)

    cq = _rms(cq_ref[...].astype(F32), gqa_ref[...]).astype(BF16)
    qall = _mm(cq, wq_ref[...])
    qsw = _mm(cq, wqsw_ref[...])
    c = _rms(lat_ref[...].astype(F32), gkva_ref[...]).astype(BF16)
    kall = _mm(c, wk_ref[...])
    v_ref[...] = (_mm(c, wv_ref[...]) + vadd_ref[...]).astype(BF16)
    vt_ref[...] = (_nt(wvt_ref[...], c) + vaddt_ref[...]).astype(BF16)
    kpe = kpe_ref[...].astype(F32)
    krot = kpesw_ref[...].astype(F32) * g2k
    qadd, kadd = qadd_ref[...], kadd_ref[...]
    for h in range(B_HEADS):
        sl = slice(h * PAD_HEAD, (h + 1) * PAD_HEAD)
        x = qall[:, sl]
        q_ref[:, sl] = ((x * g1q + qsw[:, sl] * g2q) * inv_rms(x) + qadd).astype(BF16)
        y = kall[:, sl] + kpe
        k_ref[:, sl] = ((y * g1k + krot) * inv_rms(y) + kadd).astype(BF16)


def _mla_prep(proj, lp, tables, S):
    T = proj.shape[0]
    tm = TOKEN_TILE
    nper = S // tm
    HW = B_HEADS * PAD_HEAD
    cst = lambda i: (0, 0)
    tab = pl.BlockSpec((tm, PAD_HEAD), lambda i: (i % nper, 0))
    row = pl.BlockSpec((1, PAD_HEAD), cst)
    col = lambda off: pl.BlockSpec((tm, PAD_HEAD), lambda i: (i, off // PAD_HEAD))
    out = jax.ShapeDtypeStruct((T, HW), BF16)
    return pl.pallas_call(
        _mla_prep_kernel,
        grid=(T // tm,),
        in_specs=[
            pl.BlockSpec((tm, 256), lambda i: (i, OFF_BCQ // 256)),
            col(OFF_BLAT), col(OFF_BKPE), col(OFF_BKPESW),
            pl.BlockSpec((1, B_Q_LORA), cst), pl.BlockSpec((1, B_KV_LORA), cst),
            pl.BlockSpec((B_Q_LORA, HW), cst), pl.BlockSpec((B_Q_LORA, HW), cst),
            pl.BlockSpec((B_KV_LORA, HW), cst), pl.BlockSpec((B_KV_LORA, HW), cst),
            row, row, row, row, tab, tab, row, row, pl.BlockSpec((1, HW), cst),
            pl.BlockSpec((HW, B_KV_LORA), cst), pl.BlockSpec((HW, 1), cst),
        ],
        out_specs=[pl.BlockSpec((tm, HW), lambda i: (i, 0))] * 3 + [pl.BlockSpec((HW, tm), lambda i: (i // nper, i % nper))],
        out_shape=[out, out, out, jax.ShapeDtypeStruct((T // S * HW, S), BF16)],
        compiler_params=_cparams(("parallel",)),
        name="mla_prep",
    )(proj, proj, proj, proj, lp['b_gqa'], lp['b_gkva'], lp['b_wq'], lp['b_wqsw'], lp['b_wk'], lp['b_wv'],
      lp['b_gq'], lp['b_gqsw'], lp['b_gk'], lp['b_gksw'], *tables, *lp['b_adds'], lp['b_wv'].T, lp['b_adds'][2].T)


def _flash_kernel(q_ref, k_ref, v_ref, o_ref, *, S, online):
    tq, tk = FLASH_TQ, FLASH_TK
    nk = S // tk
    heads = [slice(hh * PAD_HEAD, (hh + 1) * PAD_HEAD) for hh in range(2)]
    qs = [q_ref[:, sl] for sl in heads]
    a0 = jnp.zeros((PAD_HEAD, tq), F32) if not online else jnp.zeros((tq, PAD_HEAD), F32)

    def probs(j, hh):
        r0 = pl.multiple_of(j * tk, tk)
        return jnp.exp2(_nt(k_ref[pl.ds(r0, tk), heads[hh]], qs[hh])).astype(BF16)

    def pv(j, hh, p):
        r0 = pl.multiple_of(j * tk, tk)
        return _mm(v_ref[heads[hh], pl.ds(r0, tk)], p)

    if online:
        def body(j, carry):
            r0 = pl.multiple_of(j * tk, tk)
            out = []
            for hh, sl in enumerate(heads):
                s = _nt(qs[hh], k_ref[pl.ds(r0, tk), sl])
                m, acc = carry[hh]
                mn = jnp.maximum(m, jnp.max(s, axis=-1, keepdims=True))
                out.append((mn, acc * jnp.exp2(m - mn) + _mm(jnp.exp2(s - mn).astype(BF16), v_ref[pl.ds(r0, tk), sl])))
            return tuple(out)

        init = (jnp.full((tq, 1), MASK_VALUE, F32), a0)
        res = lax.fori_loop(0, nk, body, (init, init))
        accs = [res[0][1], res[1][1]]
    else:
        def body(j, carry):
            out = []
            for hh in range(2):
                acc, p_prev = carry[hh]
                out.append((acc + pv(j - 1, hh, p_prev), probs(j, hh)))
            return tuple(out)

        res = lax.fori_loop(1, nk, body, ((a0, probs(0, 0)), (a0, probs(0, 1))), unroll=FLASH_UNROLL)
        accs = [(res[hh][0] + pv(nk - 1, hh, res[hh][1])).T for hh in range(2)]
    for hh in range(2):
        acc = accs[hh]
        o_ref[:, hh * B_V:(hh + 1) * B_V] = (acc[:, :B_V] / acc[:, SUM_LANE:SUM_LANE + 1]).astype(BF16)


def _flash(q, k, v, B, S, online):
    T = B * S
    tq = FLASH_TQ
    nq = S // tq
    if online:
        vspec = pl.BlockSpec((S, 2 * PAD_HEAD), lambda b, h, i: (b, h))
    else:
        vspec = pl.BlockSpec((2 * PAD_HEAD, S), lambda b, h, i: (b * (B_HEADS // 2) + h, 0))
    return pl.pallas_call(
        functools.partial(_flash_kernel, S=S, online=online),
        grid=(B, B_HEADS // 2, nq),
        in_specs=[
            pl.BlockSpec((tq, 2 * PAD_HEAD), lambda b, h, i: (b * nq + i, h)),
            pl.BlockSpec((S, 2 * PAD_HEAD), lambda b, h, i: (b, h)),
            vspec,
        ],
        out_specs=pl.BlockSpec((tq, 2 * B_V), lambda b, h, i: (b * nq + i, h)),
        out_shape=jax.ShapeDtypeStruct((T, B_HEADS * B_V), BF16),
        compiler_params=_cparams(("parallel", "parallel", "parallel")),
        name="mla_flash_online" if online else "mla_flash",
    )(q, k, v)


def _hgrn_consts(L, reverse):
    levels = []
    m = 1
    while m < L:
        levels.append(m)
        m *= 2
    nl = len(levels)
    t = np.arange(L)
    wall = np.zeros((nl + 2, L, L), np.float32)
    right = np.zeros((nl, L, 1), np.float32)
    mask = np.zeros((nl + 1, L, L), np.float32)
    for li, m in enumerate(levels):
        blk = t // (2 * m)
        mid = blk * 2 * m + m
        isr = t >= mid
        for tt in range(L):
            if isr[tt]:
                wall[li, tt, mid[tt]:tt + 1] = 1.0
            else:
                wall[li, tt, tt + 1:mid[tt]] = 1.0
        right[li, :, 0] = isr
        mask[li] = (blk[:, None] == blk[None, :]) & isr[:, None] & (~isr[None, :])
    mask[nl] = np.eye(L)
    wall[nl] = np.tril(np.ones((L, L)))
    wall[nl + 1] = np.triu(np.ones((L, L)), 1)
    if reverse:
        wall = wall[:, ::-1, ::-1]
        right = right[:, ::-1]
        mask = mask[:, ::-1, ::-1]
    right = np.broadcast_to(right, (nl, L, C_DK))
    wall = wall.reshape((nl + 2) * L, L)
    return (jnp.asarray(np.ascontiguousarray(np.concatenate([wall, wall], axis=1)), BF16),
            jnp.asarray(np.ascontiguousarray(right)), jnp.asarray(np.ascontiguousarray(mask)), nl)


def _hgrn_direction(q_ref, z_ref, v_ref, lb, wall_ref, right_ref, mask_ref, st_ref, o_ref, *, L, nl, reverse):
    z = z_ref[...].astype(F32)
    sig = jax.nn.sigmoid(z)
    lf = jnp.log(jnp.maximum(lb + (1.0 - lb) * sig, TINY)) * LOG2E
    key = (1.0 - lb) * (1.0 - sig)
    qx = q_ref[...].astype(F32)
    qh = qx * jax.nn.sigmoid(qx)
    hi = lf.astype(BF16)
    lo = (lf - hi.astype(F32)).astype(BF16)
    e_all = jnp.exp2(_mm(wall_ref[...], jnp.concatenate([hi, lo], axis=0)))
    last = 0 if reverse else L - 1
    for h in range(C_HEADS):
        sl = slice(h * C_DK, (h + 1) * C_DK)
        qh_h, key_h, v_h = qh[:, sl], key[:, sl], v_ref[:, sl]
        scores = _nt(qh_h.astype(BF16), key_h.astype(BF16)) * mask_ref[nl]
        q_minus_k = qh_h - key_h
        for li in range(nl):
            x = ((key_h + right_ref[li] * q_minus_k) * e_all[li * L:(li + 1) * L, sl]).astype(BF16)
            scores = scores + _nt(x, x) * mask_ref[li]
        o = _mm(scores.astype(BF16), v_h)
        ep = e_all[nl * L:(nl + 1) * L, sl]
        er = e_all[(nl + 1) * L:(nl + 2) * L, sl]
        st = st_ref[h]
        o_ref[:, sl] = o + _nt((qh_h * ep).astype(BF16), st.astype(BF16))
        st_ref[h] = st * ep[last:last + 1] + _mm(v_h.T, (key_h * er).astype(BF16))


def _hgrn_kernel(qf_ref, zf_ref, vf_ref, qb_ref, zb_ref, vb_ref, lb_ref, wf_ref, rf_ref, mf_ref, wb_ref, rb_ref, mb_ref,
                 of_ref, ob_ref, st_ref, *, L, nl):
    @pl.when(pl.program_id(1) == 0)
    def _():
        st_ref[...] = jnp.zeros_like(st_ref)

    _hgrn_direction(qf_ref, zf_ref, vf_ref, lb_ref[0:1], wf_ref, rf_ref, mf_ref, st_ref.at[0], of_ref,
                    L=L, nl=nl, reverse=False)
    _hgrn_direction(qb_ref, zb_ref, vb_ref, lb_ref[1:2], wb_ref, rb_ref, mb_ref, st_ref.at[1], ob_ref,
                    L=L, nl=nl, reverse=True)


def _hgrn(proj, lb, consts_f, consts_b, B, S):
    T = B * S
    L = HGRN_CHUNK
    nc = S // L
    nl = consts_f[3]
    fwd = lambda b, c: b * nc + c
    bwd = lambda b, c: b * nc + nc - 1 - c
    pspec = lambda rows, off: pl.BlockSpec((L, 512), lambda b, c: (rows(b, c), off // 512))
    cst2 = lambda b, c: (0, 0)
    cst3 = lambda b, c: (0, 0, 0)
    cspecs = lambda cs: [pl.BlockSpec(cs[0].shape, cst2), pl.BlockSpec(cs[1].shape, cst3), pl.BlockSpec(cs[2].shape, cst3)]
    out = jax.ShapeDtypeStruct((T, 512), F32)
    return pl.pallas_call(
        functools.partial(_hgrn_kernel, L=L, nl=nl),
        grid=(B, nc),
        in_specs=[pspec(fwd, OFF_CQ), pspec(fwd, OFF_CFF), pspec(fwd, OFF_CI),
                  pspec(bwd, OFF_CQ), pspec(bwd, OFF_CFB), pspec(bwd, OFF_CI),
                  pl.BlockSpec((2, 512), cst2)] + cspecs(consts_f) + cspecs(consts_b),
        out_specs=[pl.BlockSpec((L, 512), lambda b, c: (fwd(b, c), 0)), pl.BlockSpec((L, 512), lambda b, c: (bwd(b, c), 0))],
        out_shape=[out, out],
        scratch_shapes=[pltpu.VMEM((2, C_HEADS, C_DV, C_DK), F32)],
        compiler_params=_cparams(("parallel", "arbitrary")),
        name="hgrn",
    )(proj, proj, proj, proj, proj, proj, lb, *consts_f[:3], *consts_b[:3])


def _fft_consts(S, B):
    N2 = FFT_N2
    N1 = S // N2
    nbat = math.gcd(B, max(1, FFT1_ROWS // N1))
    d = np.arange(D_GROUP_DIM)
    ang = 2.0 * np.pi * ((d[:, None] * d[None, :]) % D_GROUP_DIM) / D_GROUP_DIM
    wc = np.concatenate([np.cos(ang), -np.sin(ang)], axis=1) / math.sqrt(D_GROUP_DIM)
    a = np.arange(N1)
    ang1 = 2.0 * np.pi * ((a[:, None] * a[None, :]) % N1) / N1
    eye = np.eye(nbat)
    c1, s1 = np.kron(eye, np.cos(ang1)), np.kron(eye, np.sin(ang1))
    m1 = np.block([[c1, s1], [-s1, c1]])
    p1 = jnp.arange(N1, dtype=jnp.int32)[:, None, None]
    p2 = jnp.arange(N2, dtype=jnp.int32)[None, :, None]
    s2 = jnp.arange(N2, dtype=jnp.int32)[None, None, :]
    kk = (s2 * (p1 + N1 * p2)) % S
    th = kk.astype(F32) * (2.0 * math.pi / S)
    m2 = jnp.concatenate([jnp.cos(th), jnp.sin(th)], axis=2) * (1.0 / math.sqrt(S))
    return jnp.asarray(wc, BF16), jnp.asarray(m1, BF16), m2.astype(BF16), N1, nbat


def _fft1_kernel(u_ref, wc_ref, m1_ref, ar_ref, ai_ref, *, rows, ct):
    wc, m1 = wc_ref[...], m1_ref[...]
    for g in range(ct // LANES):
        sl = slice(g * LANES, (g + 1) * LANES)
        z = _mm(u_ref[:, sl], wc)
        zs = jnp.concatenate([z[:, :LANES], z[:, LANES:]], axis=0).astype(BF16)
        a = _mm(m1, zs)
        ar_ref[:, sl] = a[:rows].astype(BF16)
        ai_ref[:, sl] = a[rows:].astype(BF16)


def _fft2_kernel(ar_ref, ai_ref, m2_ref, o_ref):
    for p in range(FFT2_P):
        x = jnp.concatenate([ar_ref[p], ai_ref[p]], axis=0)
        o_ref[:, p * 512:(p + 1) * 512] = _mm(m2_ref[p], x).astype(BF16)


def _fourier(u, consts, B, S):
    wc, m1, m2, N1, nbat = consts
    N2 = FFT_N2
    ct = FFT1_COLS
    P = FFT2_P
    W = N2 * 512
    rows = nbat * N1
    u2 = u.reshape(B * N1, W)
    sds = jax.ShapeDtypeStruct((B * N1, W), BF16)
    ar, ai = pl.pallas_call(
        functools.partial(_fft1_kernel, rows=rows, ct=ct),
        grid=(B // nbat, W // ct),
        in_specs=[
            pl.BlockSpec((rows, ct), lambda b, j: (b, j)),
            pl.BlockSpec(wc.shape, lambda b, j: (0, 0)),
            pl.BlockSpec(m1.shape, lambda b, j: (0, 0)),
        ],
        out_specs=[pl.BlockSpec((rows, ct), lambda b, j: (b, j))] * 2,
        out_shape=[sds, sds],
        compiler_params=_cparams(("parallel", "parallel")),
        name="fft_stage1",
    )(u2, wc, m1)
    ar3 = ar.reshape(B * N1, N2, 512)
    ai3 = ai.reshape(B * N1, N2, 512)
    y = pl.pallas_call(
        _fft2_kernel,
        grid=(B, N1 // P),
        in_specs=[
            pl.BlockSpec((P, N2, 512), lambda b, p: (b * (N1 // P) + p, 0, 0)),
            pl.BlockSpec((P, N2, 512), lambda b, p: (b * (N1 // P) + p, 0, 0)),
            pl.BlockSpec((P, N2, 2 * N2), lambda b, p: (p, 0, 0)),
        ],
        out_specs=pl.BlockSpec((N2, P * 512), lambda b, p: (b, p)),
        out_shape=jax.ShapeDtypeStruct((B * N2, N1 * 512), BF16),
        compiler_params=_cparams(("parallel", "parallel")),
        name="fft_stage2",
    )(ar3, ai3, m2)
    return y.reshape(B * S, 512)


def _merge_kernel(x_ref, oa_ref, ob_ref, cf_ref, cb_ref, cg_ref, od_ref, g1_ref, gn_ref, wg_ref, wb_ref, wo_ref, g2_ref,
                  wr_ref, x1_ref, h2_ref, aff_ref):
    x = x_ref[...]
    h = _rms(x, g1_ref[...]).astype(BF16)
    tot = cf_ref[...] + cb_ref[...]
    gate = cg_ref[...].astype(F32)
    gate = gate * jax.nn.sigmoid(gate)
    o_c = jnp.concatenate([_rms(tot[:, hd * C_DV:(hd + 1) * C_DV], gn_ref[...]) for hd in range(C_HEADS)], axis=1)
    branches = (oa_ref[...], ob_ref[...], (o_c * gate).astype(BF16), od_ref[...])
    mixed = None
    for n in range(N_BRANCHES):
        term = jax.nn.sigmoid(_mm(h, wg_ref[n])) * _mm(branches[n], wb_ref[n])
        mixed = term if mixed is None else mixed + term
    x1 = x + _mm(mixed.astype(BF16), wo_ref[...])
    x1_ref[...] = x1
    h2 = _rms(x1, g2_ref[...]).astype(BF16)
    h2_ref[...] = h2
    logits = _nt(wr_ref[...], h2)
    mx = jnp.max(logits, axis=0, keepdims=True)
    ex = jnp.exp(logits - mx)
    aff_ref[...] = ex / jnp.sum(ex, axis=0, keepdims=True)


def _merge(x2, oa, ob, cf, cb, proj, od, g1, gn, wg, wb, wo, g2, wr_t):
    T = x2.shape[0]
    tm = TOKEN_TILE
    tok = lambda w: pl.BlockSpec((tm, w), lambda i: (i, 0))
    c2 = lambda i: (0, 0)
    c3 = lambda i: (0, 0, 0)
    return pl.pallas_call(
        _merge_kernel,
        grid=(T // tm,),
        in_specs=[
            tok(D_MODEL), tok(512), tok(512), tok(512), tok(512),
            pl.BlockSpec((tm, 512), lambda i: (i, OFF_CG // 512)), tok(512),
            pl.BlockSpec((1, D_MODEL), c2), pl.BlockSpec((1, C_DV), c2),
            pl.BlockSpec((N_BRANCHES, D_MODEL, D_MODEL), c3),
            pl.BlockSpec((N_BRANCHES, BRANCH_WIDTH, D_MODEL), c3),
            pl.BlockSpec((D_MODEL, D_MODEL), c2),
            pl.BlockSpec((1, D_MODEL), c2),
            pl.BlockSpec((N_EXPERTS, D_MODEL), c2),
        ],
        out_specs=[tok(D_MODEL), tok(D_MODEL), pl.BlockSpec((N_EXPERTS, tm), lambda i: (0, i))],
        out_shape=[jax.ShapeDtypeStruct((T, D_MODEL), F32), jax.ShapeDtypeStruct((T, D_MODEL), BF16),
                   jax.ShapeDtypeStruct((N_EXPERTS, T), F32)],
        compiler_params=_cparams(("parallel",)),
        name="merge",
    )(x2, oa, ob, cf, cb, proj, od, g1, gn, wg, wb, wo, g2, wr_t)


def _cap_rows(T):
    cap = CAPACITY_FACTOR * T // N_EXPERTS
    ntiles = T // MOE_TILE
    need = cap + BF16_SUBLANES * ntiles + max(FFN_ROWS, MOE_SUB)
    return cap, ntiles, -(-need // FFN_ROWS) * FFN_ROWS


def _route_kernel(aff_ref, tri_ref, rank_ref, cnt_ref, off_ref, *, T, cap, ntiles):
    tk = MOE_TILE
    bits = pltpu.bitcast(aff_ref[...], jnp.int32)
    capf = jnp.float32(cap)

    def count(pred):
        return jnp.sum(jnp.where(pred, 1.0, 0.0), axis=-1, keepdims=True)

    def vbody(i, v):
        cand = v | lax.shift_left(jnp.int32(1), 30 - i)
        return jnp.where(count(bits >= cand) >= capf, cand, v)

    v = lax.fori_loop(0, 31, vbody, jnp.zeros((N_EXPERTS, 1), jnp.int32))
    need = capf - count(bits > v)
    ties = bits == v
    idx = lax.broadcasted_iota(jnp.int32, (N_EXPERTS, T), 1)
    nbits = max(1, (T - 1).bit_length())

    def jbody(i, j0):
        cand = j0 | lax.shift_left(jnp.int32(1), nbits - 1 - i)
        return jnp.where(count(ties & (idx < cand)) < need, cand, j0)

    j0 = lax.fori_loop(0, nbits, jbody, jnp.zeros((N_EXPERTS, 1), jnp.int32))
    sel = (bits > v) | (ties & (idx <= j0))
    tri = tri_ref[...]
    cnt_ref[...] = jnp.zeros_like(cnt_ref)
    for i in range(ntiles):
        sl = slice(i * tk, (i + 1) * tk)
        s_i = sel[:, sl]
        sf = jnp.where(s_i, 1.0, 0.0)
        r = _mm(sf.astype(BF16), tri)
        rank_ref[:, sl] = jnp.where(s_i, r, -1.0)
        cnt_ref[:, i:i + 1] = jnp.sum(sf, axis=-1, keepdims=True)
    cnt = cnt_ref[...]
    units = jnp.ceil(cnt * (1.0 / BF16_SUBLANES))
    off_ref[...] = (_mm(units.astype(BF16), tri[:LANES, :LANES]) * BF16_SUBLANES).astype(jnp.int32)


def _route(aff_t, tri, T):
    cap, ntiles, _ = _cap_rows(T)
    assert ntiles <= LANES and MOE_TILE // BF16_SUBLANES <= 256
    full = lambda s: pl.BlockSpec(s, lambda: tuple(0 for _ in s))
    rank, cnt, off = pl.pallas_call(
        functools.partial(_route_kernel, T=T, cap=cap, ntiles=ntiles),
        in_specs=[full((N_EXPERTS, T)), full(tri.shape)],
        out_specs=[full((N_EXPERTS, T)), full((N_EXPERTS, LANES)), full((N_EXPERTS, LANES))],
        out_shape=[jax.ShapeDtypeStruct((N_EXPERTS, T), F32), jax.ShapeDtypeStruct((N_EXPERTS, LANES), F32),
                   jax.ShapeDtypeStruct((N_EXPERTS, LANES), jnp.int32)],
        compiler_params=_cparams(()),
        name="moe_route",
    )(aff_t, tri)
    return rank, cnt.astype(jnp.int32), off


def _gather_kernel(cnt_ref, off_ref, rank_ref, h_ref, xe_ref, stage_ref, extra_ref, zero_ref, sem, xsem, *, cap):
    i, n = pl.program_id(0), pl.num_programs(0)
    R = MOE_SUB
    slot = i % 2
    h = h_ref[...]
    jrow = lax.broadcasted_iota(jnp.int32, (R, MOE_TILE), 0).astype(F32)
    zrows = zero_ref.shape[0]
    nzero = (xe_ref.shape[1] - cap) // zrows

    @pl.when(i == 0)
    def _():
        zero_ref[...] = jnp.zeros_like(zero_ref)

        def zero_copy(e, k):
            return pltpu.make_async_copy(zero_ref, xe_ref.at[e, pl.ds(cap + k * zrows, zrows)], sem.at[0, e])

        for k in range(nzero):
            for e in range(N_EXPERTS):
                zero_copy(e, k).start()
            for e in range(N_EXPERTS):
                zero_copy(e, k).wait()

    def rows(e, r):
        hit = rank_ref[e:e + 1, :] == jrow + (r * R).astype(F32)
        return _mm(jnp.where(hit, 1.0, 0.0).astype(BF16), h).astype(BF16)

    def out_copy(step, sl, e):
        r0 = pl.multiple_of(off_ref[e, step], BF16_SUBLANES)
        return pltpu.make_async_copy(stage_ref.at[sl, e], xe_ref.at[e, pl.ds(r0, R)], sem.at[sl, e])

    for e in range(N_EXPERTS):
        stage_ref[slot, e] = rows(e, jnp.int32(0))

    @pl.when(i > 0)
    def _():
        for e in range(N_EXPERTS):
            out_copy(i - 1, 1 - slot, e).wait()

    for e in range(N_EXPERTS):
        out_copy(i, slot, e).start()

    for e in range(N_EXPERTS):
        def body(r, carry, e=e):
            extra_ref[...] = rows(e, r)
            r0 = pl.multiple_of(off_ref[e, i] + r * R, BF16_SUBLANES)
            cp = pltpu.make_async_copy(extra_ref, xe_ref.at[e, pl.ds(r0, R)], xsem.at[0])
            cp.start()
            cp.wait()
            return carry

        lax.fori_loop(1, (cnt_ref[e, i] + R - 1) // R, body, 0)

    @pl.when(i == n - 1)
    def _():
        for e in range(N_EXPERTS):
            out_copy(i, slot, e).wait()


def _gather(cnt, off, rank, h2, T):
    cap, ntiles, rows = _cap_rows(T)
    tk = MOE_TILE
    R = MOE_SUB
    return pl.pallas_call(
        functools.partial(_gather_kernel, cap=cap),
        grid_spec=pltpu.PrefetchScalarGridSpec(
            num_scalar_prefetch=2,
            grid=(ntiles,),
            in_specs=[
                pl.BlockSpec((N_EXPERTS, tk), lambda i, c, o: (0, i)),
                pl.BlockSpec((tk, D_MODEL), lambda i, c, o: (i, 0)),
            ],
            out_specs=pl.BlockSpec(memory_space=pl.ANY),
            scratch_shapes=[
                pltpu.VMEM((2, N_EXPERTS, R, D_MODEL), BF16),
                pltpu.VMEM((R, D_MODEL), BF16),
                pltpu.VMEM((math.gcd(rows - cap, FFN_ROWS), D_MODEL), BF16),
                pltpu.SemaphoreType.DMA((2, N_EXPERTS)),
                pltpu.SemaphoreType.DMA((1,)),
            ],
        ),
        out_shape=jax.ShapeDtypeStruct((N_EXPERTS, rows, D_MODEL), BF16),
        compiler_params=_cparams(("arbitrary",)),
        name="moe_gather",
    )(cnt, off, rank, h2)


def _ffn_kernel(used_ref, x_ref, wg_ref, wu_ref, wd_ref, y_ref, wg_s, wu_s, wd_s):
    live = pl.program_id(1) * FFN_ROWS < used_ref[pl.program_id(0)]

    @pl.when(pl.program_id(1) == 0)
    def _():
        wg_s[...] = wg_ref[0].astype(BF16)
        wu_s[...] = wu_ref[0].astype(BF16)
        wd_s[...] = wd_ref[0].astype(BF16)

    @pl.when(live)
    def _():
        x = x_ref[0]
        a = _mm(x, wg_s[...])
        u = _mm(x, wu_s[...])
        hmid = (a * jax.nn.sigmoid(a) * u).astype(BF16)
        y_ref[0] = _mm(hmid, wd_s[...]).astype(BF16)

    @pl.when(jnp.logical_not(live))
    def _():
        y_ref[...] = jnp.zeros_like(y_ref)


def _ffn(used, xe, wg, wu, wd, layer):
    E, rows, _ = xe.shape
    wspec = lambda s: pl.BlockSpec((None, 1) + s, lambda e, j, u: (layer, e, 0, 0))
    return pl.pallas_call(
        _ffn_kernel,
        grid_spec=pltpu.PrefetchScalarGridSpec(
            num_scalar_prefetch=1,
            grid=(E, rows // FFN_ROWS),
            in_specs=[
                pl.BlockSpec((1, FFN_ROWS, D_MODEL), lambda e, j, u: (e, j, 0)),
                wspec((D_MODEL, EXPERT_FF)), wspec((D_MODEL, EXPERT_FF)), wspec((EXPERT_FF, D_MODEL)),
            ],
            out_specs=pl.BlockSpec((1, FFN_ROWS, D_MODEL), lambda e, j, u: (e, j, 0)),
            scratch_shapes=[pltpu.VMEM((D_MODEL, EXPERT_FF), BF16), pltpu.VMEM((D_MODEL, EXPERT_FF), BF16),
                            pltpu.VMEM((EXPERT_FF, D_MODEL), BF16)],
        ),
        out_shape=jax.ShapeDtypeStruct(xe.shape, BF16),
        compiler_params=_cparams(("parallel", "arbitrary")),
        name="moe_ffn",
    )(used, xe, wg, wu, wd)


def _scatter_kernel(cnt_ref, off_ref, x_ref, rank_ref, aff_ref, ye_ref, o_ref, seg_ref, oh_ref, extra_ref, sem, xsem):
    i, n = pl.program_id(0), pl.num_programs(0)
    R = MOE_SUB
    tk = MOE_TILE
    slot = i % 2

    def seg_copy(step, sl, e):
        r0 = pl.multiple_of(off_ref[e, step], BF16_SUBLANES)
        return pltpu.make_async_copy(ye_ref.at[e, pl.ds(r0, R)], seg_ref.at[sl, pl.ds(e * R, R)], sem.at[sl, e])

    @pl.when(i == 0)
    def _():
        for e in range(N_EXPERTS):
            seg_copy(i, slot, e).start()

    @pl.when(i + 1 < n)
    def _():
        for e in range(N_EXPERTS):
            seg_copy(i + 1, 1 - slot, e).start()

    jlane = lax.broadcasted_iota(jnp.int32, (tk, R), 1).astype(F32)
    for e in range(N_EXPERTS):
        oh_ref[:, e * R:(e + 1) * R] = jnp.where(rank_ref[:, e:e + 1] == jlane, aff_ref[:, e:e + 1], 0.0).astype(BF16)
    for e in range(N_EXPERTS):
        seg_copy(i, slot, e).wait()
    o_ref[...] = x_ref[...] + _mm(oh_ref[...], seg_ref[slot])

    for e in range(N_EXPERTS):
        def body(r, carry, e=e):
            r0 = pl.multiple_of(off_ref[e, i] + r * R, BF16_SUBLANES)
            cp = pltpu.make_async_copy(ye_ref.at[e, pl.ds(r0, R)], extra_ref, xsem.at[0])
            cp.start()
            hit = rank_ref[:, e:e + 1] == jlane + (r * R).astype(F32)
            w = jnp.where(hit, aff_ref[:, e:e + 1], 0.0).astype(BF16)
            cp.wait()
            o_ref[...] += _mm(w, extra_ref[...])
            return carry

        lax.fori_loop(1, (cnt_ref[e, i] + R - 1) // R, body, 0)


def _scatter(cnt, off, x1, rank_t, aff_tok, ye, T):
    _, ntiles, rows = _cap_rows(T)
    tk = MOE_TILE
    R = MOE_SUB
    return pl.pallas_call(
        _scatter_kernel,
        grid_spec=pltpu.PrefetchScalarGridSpec(
            num_scalar_prefetch=2,
            grid=(ntiles,),
            in_specs=[
                pl.BlockSpec((tk, D_MODEL), lambda i, c, o: (i, 0)),
                pl.BlockSpec((tk, N_EXPERTS), lambda i, c, o: (i, 0)),
                pl.BlockSpec((tk, N_EXPERTS), lambda i, c, o: (i, 0)),
                pl.BlockSpec(memory_space=pl.ANY),
            ],
            out_specs=pl.BlockSpec((tk, D_MODEL), lambda i, c, o: (i, 0)),
            scratch_shapes=[
                pltpu.VMEM((2, N_EXPERTS * R, D_MODEL), BF16),
                pltpu.VMEM((tk, N_EXPERTS * R), BF16),
                pltpu.VMEM((R, D_MODEL), BF16),
                pltpu.SemaphoreType.DMA((2, N_EXPERTS)),
                pltpu.SemaphoreType.DMA((1,)),
            ],
        ),
        out_shape=jax.ShapeDtypeStruct((T, D_MODEL), F32),
        compiler_params=_cparams(("arbitrary",)),
        name="moe_scatter",
    )(cnt, off, x1, rank_t, aff_tok, ye)


def _expert_choice(x1, h2, aff_t, tri, wg, wu, wd, layer):
    T = x1.shape[0]
    rank, cnt, off = _route(aff_t, tri, T)
    ntiles = T // MOE_TILE
    used = off[:, ntiles - 1] + -(-cnt[:, ntiles - 1] // BF16_SUBLANES) * BF16_SUBLANES
    xe = _gather(cnt, off, rank, h2, T)
    ye = _ffn(used, xe, wg, wu, wd, layer)
    return _scatter(cnt, off, x1, rank.T, aff_t.T, ye, T)


def _pad_heads(w, n_heads, width, pad_to):
    k = w.shape[0]
    w = w.reshape(k, n_heads, width)
    return jnp.pad(w, ((0, 0), (0, 0), (0, pad_to - width))).reshape(k, n_heads * pad_to)


def _pack_layer(l, p):
    w_in = p['w_in'][l]
    edges = np.cumsum([0, 512, 128, 128, B_Q_LORA, B_KV_LORA + B_ROPE, 512, 512, 512, 512, 512, 512])
    a_q, a_k, a_v, b_cq, b_ckv, c_q, c_ff, c_fb, c_i, c_g, d_u = [w_in[:, edges[j]:edges[j + 1]] for j in range(11)]
    zeros = lambda n: jnp.zeros((D_MODEL, n), w_in.dtype)
    b_kpe = jnp.concatenate([zeros(B_NOPE), b_ckv[:, B_KV_LORA:], zeros(PAD_HEAD - B_QK)], axis=1)
    w_main = jnp.concatenate([a_q, c_q, c_ff, c_fb, c_i, c_g, a_k, a_v, b_cq, b_ckv[:, :B_KV_LORA], b_kpe,
                              _swap_rope_halves(b_kpe)], axis=1).astype(BF16)
    kv_up = p['b_kv_up'][l].reshape(B_KV_LORA, B_HEADS, B_NOPE + B_V)
    pad_g = lambda g: jnp.pad(g, (0, PAD_HEAD - B_QK)).reshape(1, PAD_HEAD)
    b_wq = _pad_heads(p['b_q_up'][l], B_HEADS, B_QK, PAD_HEAD)
    b_wqsw = _swap_rope_halves(b_wq.reshape(B_Q_LORA, B_HEADS, PAD_HEAD)).reshape(B_Q_LORA, B_HEADS * PAD_HEAD)
    b_gq, b_gk = pad_g(p['b_q_norm_g'][l]), pad_g(p['b_k_norm_g'][l])
    sink = p['a_sink'][l].astype(F32).reshape(A_KV_HEADS, A_HEADS // A_KV_HEADS)
    b_bound = math.sqrt(B_QK) * jnp.max(jnp.abs(p['b_q_norm_g'][l])) * jnp.max(jnp.abs(p['b_k_norm_g'][l]))
    lane = jnp.arange(PAD_HEAD)
    b_qadd = jnp.where(lane == SCORE_LANE, 1.0, 0.0).astype(F32).reshape(1, PAD_HEAD)
    b_kadd = jnp.where(lane == SCORE_LANE, -LOG2E * b_bound, 0.0).astype(F32).reshape(1, PAD_HEAD)
    b_vadd = jnp.tile(jnp.where(lane == SUM_LANE, 1.0, 0.0).astype(F32), B_HEADS).reshape(1, B_HEADS * PAD_HEAD)
    a_bound, a_bias, a_sink_term = _win_fast_tables(p['a_q_norm_g'][l], p['a_k_norm_g'][l], p['a_sink'][l])
    return dict(
        a_bound=a_bound, a_bias=a_bias, a_sink_term=a_sink_term,
        a_gq_t=(jnp.tile(p['a_q_norm_g'][l], A_HEADS) * (LOG2E * A_HEAD_DIM ** -0.5)).reshape(1, A_HEADS * A_HEAD_DIM),
        a_gk_t=jnp.tile(p['a_k_norm_g'][l], A_KV_HEADS).reshape(1, A_KV_HEADS * A_HEAD_DIM),
        b_bound=b_bound, b_adds=(b_qadd, b_kadd, b_vadd),
        g_mix=p['norm_mix_g'][l].reshape(1, D_MODEL), w_main=w_main, w_u=d_u.astype(BF16),
        a_gq=p['a_q_norm_g'][l].reshape(1, A_HEAD_DIM), a_gk=p['a_k_norm_g'][l].reshape(1, A_HEAD_DIM),
        a_sink=jnp.repeat(sink, WINDOW, axis=1)[:, :, None],
        b_gqa=p['b_q_a_norm_g'][l].reshape(1, B_Q_LORA), b_gkva=p['b_kv_a_norm_g'][l].reshape(1, B_KV_LORA),
        b_wq=b_wq.astype(BF16), b_wqsw=b_wqsw.astype(BF16),
        b_wk=_pad_heads(kv_up[:, :, :B_NOPE].reshape(B_KV_LORA, -1), B_HEADS, B_NOPE, PAD_HEAD).astype(BF16),
        b_wv=_pad_heads(kv_up[:, :, B_NOPE:].reshape(B_KV_LORA, -1), B_HEADS, B_V, PAD_HEAD).astype(BF16),
        b_gq=b_gq, b_gk=b_gk, b_gqsw=_swap_rope_halves(b_gq), b_gksw=_swap_rope_halves(b_gk),
        c_gn=p['c_out_norm_g'][l].reshape(1, C_DV),
        w_gate=p['w_gate'][l].astype(BF16), w_branch=p['w_branch'][l].astype(BF16), w_out=p['w_out'][l].astype(BF16),
        g_ffn=p['norm_ffn_g'][l].reshape(1, D_MODEL), w_router_t=p['w_router'][l].T.astype(BF16),
        layer=l, w_e_gate=p['w_e_gate'], w_e_up=p['w_e_up'], w_e_down=p['w_e_down'],
    )


def _layer(x2, B, S, lp, lb, cst):
    proj, u = _in_proj(x2, lp['g_mix'], lp['w_main'], lp['w_u'])
    o_a = lax.cond(lp['a_bound'] <= FAST_SOFTMAX_MAX_BOUND,
                   lambda pr: _win_attn_fast(pr, lp['a_gq_t'], lp['a_gk_t'], cst['win'], lp['a_bias'], lp['a_sink_term'], B, S),
                   lambda pr: _win_attn(pr, lp['a_gq'], lp['a_gk'], lp['a_sink'], B, S), proj)
    q, k, v, v_t = _mla_prep(proj, lp, cst['rope'][S], S)
    o_b = lax.cond(lp['b_bound'] <= FAST_SOFTMAX_MAX_BOUND,
                   lambda q, k, v, v_t: _flash(q, k, v_t, B, S, False),
                   lambda q, k, v, v_t: _flash(q, k, v, B, S, True), q, k, v, v_t)
    c_f, c_b = _hgrn(proj, lb, cst['hgrn_f'], cst['hgrn_b'], B, S)
    o_d = _fourier(u, cst['fft'][S], B, S)
    x1, h2, aff_t = _merge(x2, o_a, o_b, c_f, c_b, proj, o_d, lp['g_mix'], lp['c_gn'], lp['w_gate'], lp['w_branch'],
                           lp['w_out'], lp['g_ffn'], lp['w_router_t'])
    return _expert_choice(x1, h2, aff_t, cst['tri'], lp['w_e_gate'], lp['w_e_up'], lp['w_e_down'], lp['layer'])


def kernel(x_prompt, x_sample, norm_mix_g, w_in, a_q_norm_g, a_k_norm_g, a_sink, b_q_a_norm_g, b_q_up,
           b_kv_a_norm_g, b_kv_up, b_q_norm_g, b_k_norm_g, c_lb_logits, c_out_norm_g, w_branch, w_gate, w_out,
           norm_ffn_g, w_router, w_e_gate, w_e_up, w_e_down):
    p = dict(norm_mix_g=norm_mix_g, w_in=w_in, a_q_norm_g=a_q_norm_g, a_k_norm_g=a_k_norm_g, a_sink=a_sink,
             b_q_a_norm_g=b_q_a_norm_g, b_q_up=b_q_up, b_kv_a_norm_g=b_kv_a_norm_g, b_kv_up=b_kv_up,
             b_q_norm_g=b_q_norm_g, b_k_norm_g=b_k_norm_g, c_out_norm_g=c_out_norm_g, w_branch=w_branch,
             w_gate=w_gate, w_out=w_out, norm_ffn_g=norm_ffn_g, w_router=w_router, w_e_gate=w_e_gate,
             w_e_up=w_e_up, w_e_down=w_e_down)
    lb_w = jax.nn.softmax(c_lb_logits.astype(F32), axis=0)
    lb_all = jnp.cumsum(lb_w, axis=0) - lb_w[0:1]
    groups = [x_prompt, x_sample]
    seqs = sorted({g.shape[1] for g in groups})
    tk = MOE_TILE
    cst = dict(
        rope={S: _rope_tables(S) for S in seqs},
        fft={g.shape[1]: _fft_consts(g.shape[1], g.shape[0]) for g in groups},
        hgrn_f=_hgrn_consts(HGRN_CHUNK, False), hgrn_b=_hgrn_consts(HGRN_CHUNK, True),
        win=_win_fast_consts(),
        tri=jnp.asarray(np.triu(np.ones((tk, tk), np.float32), 1), BF16),
    )
    layers = [_pack_layer(l, p) for l in range(DEPTH)]
    outs = []
    for xg in groups:
        B, S, _ = xg.shape
        y = xg.reshape(B * S, D_MODEL)
        for l in range(DEPTH):
            y = _layer(y, B, S, layers[l], lb_all[l], cst)
        outs.append(y.reshape(B, S, D_MODEL))
    return tuple(outs)
```
